```python
import math
import jax, jax.numpy as jnp
from jax import lax
import numpy as np

D_MODEL = 1024
BATCH = 2
SEQ = 8192
DEPTH = 4
DEC_BATCH = 32
DEC_SEQ = 1
PAST_LEN = 8192
PAGE_SIZE = 128

D_SSD = D_MODEL
SSD_HEAD_DIM = 64
SSD_HEADS = D_SSD // SSD_HEAD_DIM
SSD_STATE = 128
SSD_GROUPS = 2
SSD_CONV = 4
SSD_CHUNK = 128
CONV_DIM = D_SSD + 2 * SSD_GROUPS * SSD_STATE

ATT_HEAD_DIM = 64
ATT_V_DIM = 2 * ATT_HEAD_DIM
ATT_HEADS = D_MODEL // ATT_V_DIM
D_ATT = ATT_HEADS * ATT_V_DIM
QK_DIM = ATT_HEADS * 2 * ATT_HEAD_DIM
ATT_SCALE = ATT_HEAD_DIM ** -0.5
Q_BLOCK = 128

D_MIX = D_SSD + D_ATT
IN_SPLITS = (D_SSD, CONV_DIM, SSD_HEADS, QK_DIM, QK_DIM, D_ATT, D_ATT)
D_IN_PROJ = D_SSD + CONV_DIM + SSD_HEADS + 2 * QK_DIM + 2 * D_ATT
PLE_DIM = 256
DEEPNORM_ALPHA = (2 * DEPTH) ** 0.25
DEEPNORM_BETA = (8 * DEPTH) ** -0.25
NORM_EPS = 1e-5

kernel_name = "hymba_ssd_diffattn_deepnorm_ple_step"


def layer_norm(x, g, b):
    xf = x.astype(jnp.float32)
    mu = jnp.mean(xf, -1, keepdims=True)
    var = jnp.mean(jnp.square(xf - mu), -1, keepdims=True)
    return ((xf - mu) * lax.rsqrt(var + NORM_EPS)).astype(x.dtype) * g + b


def rms_norm(x, w):
    xf = x.astype(jnp.float32)
    return (xf * lax.rsqrt(jnp.mean(jnp.square(xf), -1, keepdims=True) + NORM_EPS)).astype(x.dtype) * w


def split_in_proj(h):
    offsets = np.cumsum(IN_SPLITS)[:-1].tolist()
    return jnp.split(h, offsets, axis=-1)


def causal_dwconv(xbc, prev, w, b):
    xpad = jnp.concatenate([prev.astype(xbc.dtype), xbc], axis=1)
    y = lax.conv_general_dilated(xpad, w[:, None, :], window_strides=(1,), padding='VALID',
                                 dimension_numbers=('NWC', 'WIO', 'NWC'), feature_group_count=CONV_DIM)
    return jax.nn.silu(y + b), xpad[:, -(SSD_CONV - 1):]


def ssd_scan(x, dt, A, B, C, h0):
    b, L, H, P = x.shape
    G, N = B.shape[2], B.shape[3]
    R = H // G
    cl = SSD_CHUNK if L % SSD_CHUNK == 0 else L
    nc = L // cl
    f32 = jnp.float32
    xc = x.astype(f32).reshape(b, nc, cl, G, R, P)
    Bc = B.astype(f32).reshape(b, nc, cl, G, N)
    Cc = C.astype(f32).reshape(b, nc, cl, G, N)
    dtc = dt.reshape(b, nc, cl, G, R)
    acum = jnp.cumsum(dtc * A.reshape(G, R), axis=2)
    at = jnp.moveaxis(acum, 2, -1)
    causal = jnp.tril(jnp.ones((cl, cl), dtype=bool))
    decay = jnp.exp(jnp.where(causal, at[..., :, None] - at[..., None, :], -jnp.inf))
    cb = jnp.einsum('bcign,bcjgn->bcgij', Cc, Bc)
    w = cb[:, :, :, None] * decay * jnp.moveaxis(dtc, 2, -1)[..., None, :]
    y_diag = jnp.einsum('bcgrij,bcjgrp->bcigrp', w, xc)
    xw = xc * (jnp.exp(acum[:, :, -1:] - acum) * dtc)[..., None]
    states = jnp.einsum('bcjgn,bcjgrp->bcgrpn', Bc, xw)
    chunk_decay = jnp.exp(acum[:, :, -1])

    def step(h, inp):
        s_c, d_c = inp
        return d_c[..., None, None] * h + s_c, h

    h_last, h_in = lax.scan(step, h0.astype(f32).reshape(b, G, R, P, N),
                            (jnp.moveaxis(states, 1, 0), jnp.moveaxis(chunk_decay, 1, 0)))
    h_in = jnp.moveaxis(h_in, 0, 1)
    y_off = jnp.einsum('bcign,bcgrpn->bcigrp', Cc, h_in) * jnp.exp(acum)[..., None]
    y = (y_diag + y_off).reshape(b, L, H, P)
    return y, h_last.reshape(b, H, P, N).astype(h0.dtype)


def ssd_branch(z, xbc_raw, dt_raw, conv_prev, h0, conv_w, conv_b, dt_bias, a_log, d_skip, norm_w):
    b, L, _ = z.shape
    xbc, conv_new = causal_dwconv(xbc_raw, conv_prev, conv_w, conv_b)
    xs, Bm, Cm = jnp.split(xbc, [D_SSD, D_SSD + SSD_GROUPS * SSD_STATE], axis=-1)
    xs = xs.reshape(b, L, SSD_HEADS, SSD_HEAD_DIM)
    Bm = Bm.reshape(b, L, SSD_GROUPS, SSD_STATE)
    Cm = Cm.reshape(b, L, SSD_GROUPS, SSD_STATE)
    dt = jax.nn.softplus(dt_raw.astype(jnp.float32) + dt_bias.astype(jnp.float32))
    A = -jnp.exp(a_log.astype(jnp.float32))
    y, h_final = ssd_scan(xs, dt, A, Bm, Cm, h0)
    y = (y + d_skip.astype(jnp.float32)[:, None] * xs.astype(jnp.float32)).astype(z.dtype)
    y = y.reshape(b, L, D_SSD) * jax.nn.silu(z)
    y = rms_norm(y.reshape(b, L, SSD_GROUPS, D_SSD // SSD_GROUPS),
                 norm_w.reshape(SSD_GROUPS, D_SSD // SSD_GROUPS)).reshape(b, L, D_SSD)
    return y, conv_new, h_final


def diff_lambda(lq1, lk1, lq2, lk2, lam_init):
    f32 = jnp.float32
    return (jnp.exp(jnp.sum(lq1.astype(f32) * lk1.astype(f32)))
            - jnp.exp(jnp.sum(lq2.astype(f32) * lk2.astype(f32))) + lam_init)


def diff_attn_prompt(q, k, v, lam):
    b, L = q.shape[:2]
    kpos = jnp.arange(L)

    def block(i):
        qb = lax.dynamic_slice_in_dim(q, i * Q_BLOCK, Q_BLOCK, axis=1)
        s = jnp.einsum('bqhcd,bkhcd->bhcqk', qb, k).astype(jnp.float32) * ATT_SCALE
        qpos = i * Q_BLOCK + jnp.arange(Q_BLOCK)
        s = jnp.where(kpos[None, :] <= qpos[:, None], s, -jnp.inf)
        pr = jax.nn.softmax(s, axis=-1)
        a = (pr[:, :, 0] - lam * pr[:, :, 1]).astype(v.dtype)
        return jnp.einsum('bhqk,bkhe->bqhe', a, v)

    o = lax.map(block, jnp.arange(L // Q_BLOCK))
    return jnp.moveaxis(o, 0, 1).reshape(b, L, ATT_HEADS, ATT_V_DIM)


def diff_attn_sample(q, k_new, v_new, k_past, v_past, lam):
    S = q.shape[1]
    P = k_past.shape[1]
    s_past = jnp.einsum('bqhcd,bkhcd->bhcqk', q, k_past).astype(jnp.float32) * ATT_SCALE
    s_new = jnp.einsum('bqhcd,bkhcd->bhcqk', q, k_new).astype(jnp.float32) * ATT_SCALE
    s_new = jnp.where(jnp.tril(jnp.ones((S, S), dtype=bool)), s_new, -jnp.inf)
    pr = jax.nn.softmax(jnp.concatenate([s_past, s_new], axis=-1), axis=-1)
    a = (pr[:, :, 0] - lam * pr[:, :, 1]).astype(v_new.dtype)
    return (jnp.einsum('bhqk,bkhe->bqhe', a[..., :P], v_past)
            + jnp.einsum('bhqk,bkhe->bqhe', a[..., P:], v_new))


def attn_gate(o, z_att, subln_w, lam_init):
    b, L = o.shape[:2]
    o = rms_norm(o, subln_w) * (1.0 - lam_init)
    return o.reshape(b, L, D_ATT) * jax.nn.silu(z_att)


def layer_out(x, y_ssd, y_att, p, w_out, ln_g, ln_b, w_ple, w_ple_gate, b_ple_gate):
    y = jnp.concatenate([y_ssd, y_att], axis=-1) @ w_out
    h = layer_norm(DEEPNORM_ALPHA * x + y, ln_g, ln_b)
    gate = jax.nn.sigmoid(h @ w_ple_gate + b_ple_gate)
    return h + gate * (p @ w_ple)


def setup_inputs(seed: int = 0) -> dict:
    key = jax.random.key(seed)
    ks = jax.random.split(key, 32)
    f32 = jnp.float32
    n_pages = PAST_LEN // PAGE_SIZE
    used = DEC_BATCH * n_pages
    n_pool = used + max(1, used // 4)

    def nrm(k, shape, s):
        return jax.random.normal(k, shape, f32) * s

    dt0 = jnp.exp(jax.random.uniform(ks[12], (DEPTH, SSD_HEADS), f32)
                  * (math.log(0.1) - math.log(1e-3)) + math.log(1e-3))
    return {
        "x_prompt": nrm(ks[0], (BATCH, SEQ, D_MODEL), 1.0),
        "x_sample": nrm(ks[1], (DEC_BATCH, DEC_SEQ, D_MODEL), 1.0),
        "cache_k": nrm(ks[2], (DEPTH, n_pool, PAGE_SIZE, ATT_HEADS, 2, ATT_HEAD_DIM), 1.0),
        "cache_v": nrm(ks[3], (DEPTH, n_pool, PAGE_SIZE, ATT_HEADS, ATT_V_DIM), 1.0),
        "state_ssm": nrm(ks[4], (DEPTH, DEC_BATCH, SSD_HEADS, SSD_HEAD_DIM, SSD_STATE), 0.1),
        "state_conv": nrm(ks[5], (DEPTH, DEC_BATCH, SSD_CONV - 1, CONV_DIM), 1.0),
        "page_table": jax.random.permutation(ks[6], n_pool)[:used].reshape(DEC_BATCH, n_pages).astype(jnp.int32),
        "p_prompt": nrm(ks[7], (DEPTH, BATCH, SEQ, PLE_DIM), 1.0),
        "p_sample": nrm(ks[8], (DEPTH, DEC_BATCH, DEC_SEQ, PLE_DIM), 1.0),
        "w_in": nrm(ks[9], (DEPTH, D_MODEL, D_IN_PROJ), D_MODEL ** -0.5),
        "conv_w": nrm(ks[10], (DEPTH, SSD_CONV, CONV_DIM), SSD_CONV ** -0.5),
        "conv_b": nrm(ks[11], (DEPTH, CONV_DIM), 0.02),
        "dt_bias": dt0 + jnp.log(-jnp.expm1(-dt0)),
        "a_log": jnp.log(jax.random.uniform(ks[13], (DEPTH, SSD_HEADS), f32, 1.0, 16.0)),
        "d_skip": 1.0 + nrm(ks[14], (DEPTH, SSD_HEADS), 0.02),
        "ssd_norm_w": 1.0 + nrm(ks[15], (DEPTH, D_SSD), 0.02),
        "lam_q1": nrm(ks[16], (DEPTH, ATT_HEAD_DIM), 0.1),
        "lam_k1": nrm(ks[17], (DEPTH, ATT_HEAD_DIM), 0.1),
        "lam_q2": nrm(ks[18], (DEPTH, ATT_HEAD_DIM), 0.1),
        "lam_k2": nrm(ks[19], (DEPTH, ATT_HEAD_DIM), 0.1),
        "subln_w": 1.0 + nrm(ks[20], (DEPTH, ATT_V_DIM), 0.02),
        "w_out": nrm(ks[21], (DEPTH, D_MIX, D_MODEL), D_MIX ** -0.5 * DEEPNORM_BETA),
        "ln_g": 1.0 + nrm(ks[22], (DEPTH, D_MODEL), 0.02),
        "ln_b": nrm(ks[23], (DEPTH, D_MODEL), 0.02),
        "w_ple": nrm(ks[24], (DEPTH, PLE_DIM, D_MODEL), PLE_DIM ** -0.5),
        "w_ple_gate": nrm(ks[25], (DEPTH, D_MODEL, D_MODEL), D_MODEL ** -0.5),
        "b_ple_gate": nrm(ks[26], (DEPTH, D_MODEL), 0.02),
    }


def reference(x_prompt, x_sample, cache_k, cache_v, state_ssm, state_conv, page_table, p_prompt, p_sample,
              w_in, conv_w, conv_b, dt_bias, a_log, d_skip, ssd_norm_w, lam_q1, lam_k1, lam_q2, lam_k2,
              subln_w, w_out, ln_g, ln_b, w_ple, w_ple_gate, b_ple_gate):
    xp, xs = x_prompt, x_sample
    b, L = xp.shape[:2]
    db, S = xs.shape[:2]
    past = page_table.shape[1] * cache_k.shape[2]
    kp, vp, hp, cp, ksl, vsl, hsl, csl = [], [], [], [], [], [], [], []
    for l in range(DEPTH):
        lam_init = 0.8 - 0.6 * math.exp(-0.3 * l)
        lam = diff_lambda(lam_q1[l], lam_k1[l], lam_q2[l], lam_k2[l], lam_init)
        ssd_w = (conv_w[l], conv_b[l], dt_bias[l], a_log[l], d_skip[l], ssd_norm_w[l])
        out_w = (w_out[l], ln_g[l], ln_b[l], w_ple[l], w_ple_gate[l], b_ple_gate[l])

        z, xbc, dt, q, k, v, za = split_in_proj(xp @ w_in[l])
        y_ssd, conv_new, h_new = ssd_branch(
            z, xbc, dt, jnp.zeros((b, SSD_CONV - 1, CONV_DIM), xp.dtype),
            jnp.zeros((b, SSD_HEADS, SSD_HEAD_DIM, SSD_STATE), jnp.float32), *ssd_w)
        q = q.reshape(b, L, ATT_HEADS, 2, ATT_HEAD_DIM)
        k = k.reshape(b, L, ATT_HEADS, 2, ATT_HEAD_DIM)
        v = v.reshape(b, L, ATT_HEADS, ATT_V_DIM)
        y_att = attn_gate(diff_attn_prompt(q, k, v, lam), za, subln_w[l], lam_init)
        xp = layer_out(xp, y_ssd, y_att, p_prompt[l], *out_w)
        kp.append(k); vp.append(v); hp.append(h_new); cp.append(conv_new)

        z, xbc, dt, q, k, v, za = split_in_proj(xs @ w_in[l])
        y_ssd, conv_new, h_new = ssd_branch(z, xbc, dt, state_conv[l], state_ssm[l], *ssd_w)
        q = q.reshape(db, S, ATT_HEADS, 2, ATT_HEAD_DIM)
        k = k.reshape(db, S, ATT_HEADS, 2, ATT_HEAD_DIM)
        v = v.reshape(db, S, ATT_HEADS, ATT_V_DIM)
        k_past = cache_k[l, page_table].reshape(db, past, ATT_HEADS, 2, ATT_HEAD_DIM)
        v_past = cache_v[l, page_table].reshape(db, past, ATT_HEADS, ATT_V_DIM)
        y_att = attn_gate(diff_attn_sample(q, k, v, k_past, v_past, lam), za, subln_w[l], lam_init)
        xs = layer_out(xs, y_ssd, y_att, p_sample[l], *out_w)
        ksl.append(k); vsl.append(v); hsl.append(h_new); csl.append(conv_new)

    return (xp, xs, jnp.stack(kp), jnp.stack(vp), jnp.stack(hp), jnp.stack(cp),
            jnp.stack(ksl), jnp.stack(vsl), jnp.stack(hsl), jnp.stack(csl))
```

```python
import functools
import math

import numpy as np
import jax
import jax.numpy as jnp
from jax import lax
from jax.experimental import pallas as pl
from jax.experimental.pallas import tpu as pltpu

F32 = jnp.float32
BF16 = jnp.bfloat16

D_MODEL = 1024
DEPTH = 4
D_SSD = 1024
SSD_HEAD_DIM = 64
SSD_HEADS = 16
SSD_STATE = 128
SSD_GROUPS = 2
SSD_CONV = 4
SSD_CHUNK = 128
CONV_DIM = D_SSD + 2 * SSD_GROUPS * SSD_STATE
ATT_HEAD_DIM = 64
ATT_V_DIM = 128
ATT_HEADS = 8
D_ATT = 1024
QK_DIM = 1024
ATT_SCALE = ATT_HEAD_DIM ** -0.5
PLE_DIM = 256
DEEPNORM_ALPHA = (2 * DEPTH) ** 0.25
NORM_EPS = 1e-5
LANES = 128
DT_PAD = LANES
C_Z, C_XBC, C_Q, C_K, C_V, C_ZA, C_DT, C_END = 0, 1024, 2560, 3584, 4608, 5632, 6656, 6784
VMEM_LIMIT = 56 * 1024 * 1024


def _cparams(sem):
    return pltpu.CompilerParams(dimension_semantics=sem, vmem_limit_bytes=VMEM_LIMIT)


def _silu(x):
    return x * jax.nn.sigmoid(x)


def _softplus(x):
    return jnp.maximum(x, 0.0) + jnp.log1p(jnp.exp(-jnp.abs(x)))


def _split3(a):
    hi = a.astype(BF16)
    r1 = a - hi.astype(F32)
    mid = r1.astype(BF16)
    lo = (r1 - mid.astype(F32)).astype(BF16)
    return hi, mid, lo


def _dot(a, b):
    return jnp.dot(a, b, preferred_element_type=F32)


def _dot_nt(a, b):
    return lax.dot_general(a, b, (((1,), (1,)), ((), ())), preferred_element_type=F32)


def _dot_sel_rhs(a, sel):
    hi, mid, lo = _split3(a)
    return (_dot(hi, sel) + _dot(mid, sel)) + _dot(lo, sel)


def _dot_sel_lhs(sel, b):
    hi, mid, lo = _split3(b)
    return (_dot(sel, hi) + _dot(sel, mid)) + _dot(sel, lo)


def _in_proj_kernel(x_ref, w_ref, z_ref, xbc_ref, dt_ref, q_ref, k_ref, v_ref, za_ref, kb_ref, vb_ref):
    xb = x_ref[...].astype(BF16)

    def mm(lo, hi):
        return _dot(xb, w_ref[:, lo:hi])

    z_ref[...] = mm(C_Z, C_XBC)
    xbc_ref[...] = mm(C_XBC, C_Q)
    q_ref[...] = (mm(C_Q, C_K) * ATT_SCALE).astype(BF16)
    k = mm(C_K, C_V)
    k_ref[...] = k
    kb_ref[...] = k.astype(BF16)
    v = mm(C_V, C_ZA)
    v_ref[...] = v
    vb_ref[...] = v.astype(BF16)
    za_ref[...] = mm(C_ZA, C_DT)
    dt_ref[...] = mm(C_DT, C_END)


def _in_proj(x, w_r, tm):
    m = x.shape[0]
    assert m % tm == 0
    widths = (1024, CONV_DIM, DT_PAD, 1024, 1024, 1024, 1024, 1024, 1024)
    dtypes = (F32, F32, F32, BF16, F32, F32, F32, BF16, BF16)
    out_shape = tuple(jax.ShapeDtypeStruct((m, w), d) for w, d in zip(widths, dtypes))
    out_specs = tuple(pl.BlockSpec((tm, w), lambda i: (i, 0)) for w in widths)
    return pl.pallas_call(
        _in_proj_kernel,
        out_shape=out_shape,
        grid=(m // tm,),
        in_specs=[pl.BlockSpec((tm, D_MODEL), lambda i: (i, 0)),
                  pl.BlockSpec((D_MODEL, C_END), lambda i: (0, 0))],
        out_specs=out_specs,
        compiler_params=_cparams(("arbitrary",)),
        name="in_proj",
    )(x, w_r)


def _layer_out_kernel(ys_ref, ya_ref, x_ref, p_ref, wo_ref, g_ref, b_ref, wp_ref, wg_ref, bg_ref, o_ref):
    y = _dot(ys_ref[...], wo_ref[0:D_SSD, :]) + _dot(ya_ref[...], wo_ref[D_SSD:, :])
    t = DEEPNORM_ALPHA * x_ref[...] + y
    mu = jnp.mean(t, axis=-1, keepdims=True)
    var = jnp.mean(jnp.square(t - mu), axis=-1, keepdims=True)
    h = ((t - mu) * lax.rsqrt(var + NORM_EPS)) * g_ref[...] + b_ref[...]
    gate = jax.nn.sigmoid(_dot(h.astype(BF16), wg_ref[...]) + bg_ref[...])
    o_ref[...] = h + gate * _dot(p_ref[...].astype(BF16), wp_ref[...])


def _layer_out(ys, ya, x, p, wo, g, b, wp, wg, bg, tm):
    m = x.shape[0]
    assert m % tm == 0
    row = lambda w: pl.BlockSpec((tm, w), lambda i: (i, 0))
    full = lambda a: pl.BlockSpec(a.shape, lambda i: (0, 0))
    return pl.pallas_call(
        _layer_out_kernel,
        out_shape=jax.ShapeDtypeStruct((m, D_MODEL), F32),
        grid=(m // tm,),
        in_specs=[row(D_SSD), row(D_ATT), row(D_MODEL), row(PLE_DIM),
                  full(wo), full(g), full(b), full(wp), full(wg), full(bg)],
        out_specs=row(D_MODEL),
        compiler_params=_cparams(("arbitrary",)),
        name="layer_out",
    )(ys, ya, x, p, wo, g, b, wp, wg, bg)


def _head_expand_matrix():
    e = np.zeros((LANES, D_SSD), np.float32)
    for h in range(SSD_HEADS):
        e[h, h * SSD_HEAD_DIM:(h + 1) * SSD_HEAD_DIM] = 1.0
    return e


def _ssd_prompt_kernel(xbc_ref, dt_ref, z_ref, cw_ref, cb_ref, dtb_ref, alog_ref, dsk_ref, nw_ref,
                       exp_ref, expt_ref, y_ref, h_ref, xpad_ref):
    cl = SSD_CHUNK
    c = pl.program_id(1)

    @pl.when(c == 0)
    def _():
        xpad_ref[0:8, :] = jnp.zeros((8, CONV_DIM), F32)
        h_ref[...] = jnp.zeros(h_ref.shape, F32)

    xpad_ref[8:8 + cl, :] = xbc_ref[0]
    acc = cb_ref[...] + cw_ref[0:1, :] * xpad_ref[5:5 + cl, :]
    acc = acc + cw_ref[1:2, :] * xpad_ref[6:6 + cl, :]
    acc = acc + cw_ref[2:3, :] * xpad_ref[7:7 + cl, :]
    acc = acc + cw_ref[3:4, :] * xpad_ref[8:8 + cl, :]
    xpad_ref[5:8, :] = xpad_ref[5 + cl:8 + cl, :]
    act = _silu(acc)
    xs = act[:, 0:D_SSD]
    gn = SSD_STATE
    b_g = [act[:, D_SSD + g * gn:D_SSD + (g + 1) * gn] for g in range(SSD_GROUPS)]
    c_g = [act[:, D_SSD + (SSD_GROUPS + g) * gn:D_SSD + (SSD_GROUPS + g + 1) * gn] for g in range(SSD_GROUPS)]

    dt = _softplus(dt_ref[0] + dtb_ref[...])
    a_neg = -jnp.exp(alog_ref[...])
    d_a = dt * a_neg
    row = lax.broadcasted_iota(jnp.int32, (cl, cl), 0)
    col = lax.broadcasted_iota(jnp.int32, (cl, cl), 1)
    causal = col <= row
    tril = jnp.where(causal, 1.0, 0.0).astype(BF16)
    acum = _dot_sel_lhs(tril, d_a)
    acum_t = acum.T
    dt_t = dt.T
    a_last = acum[cl - 1:cl, :]

    e_mat = exp_ref[...]
    e_acum = _dot_sel_rhs(jnp.exp(acum), e_mat)
    e_rest = _dot_sel_rhs(jnp.exp(a_last - acum) * dt, e_mat)
    xw = xs * e_rest
    cd_col = jnp.broadcast_to(jnp.exp(acum_t[:, cl - 1:cl]), (LANES, LANES))
    cd_full = _dot_sel_lhs(expt_ref[...], cd_col)

    lane = lax.broadcasted_iota(jnp.int32, (cl, LANES), 1)
    lo_half = lane < SSD_HEAD_DIM
    heads_per_group = SSD_HEADS // SSD_GROUPS
    gw = heads_per_group * SSD_HEAD_DIM
    y_parts = []
    for g in range(SSD_GROUPS):
        cbf = c_g[g].astype(BF16)
        bbf = b_g[g].astype(BF16)
        cb = _dot_nt(cbf, bbf)
        h_old = h_ref[0, g * gw:(g + 1) * gw, :]
        y_off = _dot_nt(cbf, h_old.astype(BF16))
        for hp in range(heads_per_group // 2):
            x_pair = xs[:, g * gw + hp * LANES:g * gw + (hp + 1) * LANES]
            pair = None
            for s in range(2):
                h = g * heads_per_group + 2 * hp + s
                seg = acum[:, h:h + 1] - acum_t[h:h + 1, :]
                decay = jnp.exp(jnp.where(causal, seg, -jnp.inf))
                w = (cb * decay) * dt_t[h:h + 1, :]
                x_half = jnp.where(lo_half if s == 0 else jnp.logical_not(lo_half), x_pair, 0.0)
                yd = _dot(w.astype(BF16), x_half.astype(BF16))
                pair = yd if pair is None else pair + yd
            y_parts.append(pair + y_off[:, hp * LANES:(hp + 1) * LANES]
                           * e_acum[:, g * gw + hp * LANES:g * gw + (hp + 1) * LANES])
        st = _dot(xw[:, g * gw:(g + 1) * gw].T.astype(BF16), bbf)
        h_ref[0, g * gw:(g + 1) * gw, :] = cd_full[g * gw:(g + 1) * gw, :] * h_old + st

    y = jnp.concatenate(y_parts, axis=1) + dsk_ref[...] * xs
    y = y * _silu(z_ref[0])
    outs = []
    for g in range(SSD_GROUPS):
        yg = y[:, g * gw:(g + 1) * gw]
        ms = jnp.mean(jnp.square(yg), axis=-1, keepdims=True)
        outs.append(yg * lax.rsqrt(ms + NORM_EPS) * nw_ref[:, g * gw:(g + 1) * gw])
    y_ref[0] = jnp.concatenate(outs, axis=1).astype(y_ref.dtype)


def _ssd_prompt(xbc, dt, z, cw, cb, dtb, alog, dsk, nw, e_mat, e_mat_t):
    b, l, _ = xbc.shape
    assert l % SSD_CHUNK == 0
    nc = l // SSD_CHUNK
    tok = lambda w: pl.BlockSpec((1, SSD_CHUNK, w), lambda i, j: (i, j, 0))
    full = lambda a: pl.BlockSpec(a.shape, lambda i, j: (0,) * a.ndim)
    return pl.pallas_call(
        _ssd_prompt_kernel,
        out_shape=(jax.ShapeDtypeStruct((b, l, D_SSD), BF16),
                   jax.ShapeDtypeStruct((b, SSD_HEADS * SSD_HEAD_DIM, SSD_STATE), F32)),
        grid=(b, nc),
        in_specs=[tok(CONV_DIM), tok(DT_PAD), tok(D_SSD), full(cw), full(cb), full(dtb), full(alog),
                  full(dsk), full(nw), full(e_mat), full(e_mat_t)],
        out_specs=(tok(D_SSD),
                   pl.BlockSpec((1, SSD_HEADS * SSD_HEAD_DIM, SSD_STATE), lambda i, j: (i, 0, 0))),
        scratch_shapes=[pltpu.VMEM((8 + SSD_CHUNK, CONV_DIM), F32)],
        compiler_params=_cparams(("arbitrary", "arbitrary")),
        name="ssd_prompt",
    )(xbc, dt, z, cw, cb, dtb, alog, dsk, nw, e_mat, e_mat_t)


def _lambda_from(lam_ref):
    s1 = jnp.sum(lam_ref[0:1, :] * lam_ref[1:2, :], axis=-1, keepdims=True)
    s2 = jnp.sum(lam_ref[2:3, :] * lam_ref[3:4, :], axis=-1, keepdims=True)
    lam_init = lam_ref[4:5, 0:1]
    return jnp.exp(s1) - jnp.exp(s2) + lam_init, lam_init


def _attn_prompt_kernel(q_ref, k_ref, v_ref, za_ref, lam_ref, sw_ref, o_ref, qs_ref, m_ref, l_ref, acc_ref, *, tq):
    qi = pl.program_id(2)
    q = q_ref[0]
    lane = lax.broadcasted_iota(jnp.int32, (tq, LANES), 1)
    zero = jnp.zeros_like(q)
    qs_ref[0:tq, :] = jnp.where(lane < ATT_HEAD_DIM, q, zero)
    qs_ref[tq:2 * tq, :] = jnp.where(lane >= ATT_HEAD_DIM, q, zero)
    m_ref[...] = jnp.full(m_ref.shape, -jnp.inf, F32)
    l_ref[...] = jnp.zeros(l_ref.shape, F32)
    acc_ref[...] = jnp.zeros(acc_ref.shape, F32)

    def block(ki, masked):
        start = pl.multiple_of(ki * tq, tq)
        k = k_ref[0, pl.ds(start, tq), :]
        v = v_ref[0, pl.ds(start, tq), :]
        s = _dot_nt(qs_ref[...], k)
        if masked:
            r = lax.broadcasted_iota(jnp.int32, (2 * tq, tq), 0)
            r = jnp.where(r >= tq, r - tq, r)
            cidx = lax.broadcasted_iota(jnp.int32, (2 * tq, tq), 1)
            s = jnp.where(cidx <= r, s, -jnp.inf)
        m_prev = m_ref[...]
        m_new = jnp.maximum(m_prev, jnp.max(s, axis=-1, keepdims=True))
        alpha = jnp.exp(m_prev - m_new)
        p = jnp.exp(s - m_new)
        l_ref[...] = alpha * l_ref[...] + jnp.sum(p, axis=-1, keepdims=True)
        acc_ref[...] = alpha * acc_ref[...] + _dot(p.astype(BF16), v)
        m_ref[...] = m_new

    def body(ki, carry):
        block(ki, False)
        return carry

    lax.fori_loop(0, qi, body, 0)
    block(qi, True)

    o = acc_ref[...] / l_ref[...]
    lam, lam_init = _lambda_from(lam_ref)
    od = o[0:tq, :] - lam * o[tq:2 * tq, :]
    ms = jnp.mean(jnp.square(od), axis=-1, keepdims=True)
    on = od * lax.rsqrt(ms + NORM_EPS) * sw_ref[...] * (1.0 - lam_init)
    o_ref[0] = (on * _silu(za_ref[0])).astype(o_ref.dtype)


def _attn_prompt(q, kb, vb, za, lam_blk, sw, tq):
    b, l, _ = q.shape
    assert l % tq == 0
    nq = l // tq
    qspec = pl.BlockSpec((1, tq, LANES), lambda bi, h, i: (bi, i, h))
    kspec = pl.BlockSpec((1, l, LANES), lambda bi, h, i: (bi, 0, h))
    full = lambda a: pl.BlockSpec(a.shape, lambda bi, h, i: (0, 0))
    return pl.pallas_call(
        functools.partial(_attn_prompt_kernel, tq=tq),
        out_shape=jax.ShapeDtypeStruct((b, l, D_ATT), BF16),
        grid=(b, ATT_HEADS, nq),
        in_specs=[qspec, kspec, kspec, qspec, full(lam_blk), full(sw)],
        out_specs=qspec,
        scratch_shapes=[pltpu.VMEM((2 * tq, LANES), BF16), pltpu.VMEM((2 * tq, 1), F32),
                        pltpu.VMEM((2 * tq, 1), F32), pltpu.VMEM((2 * tq, ATT_V_DIM), F32)],
        compiler_params=_cparams(("arbitrary", "arbitrary", "arbitrary")),
        name="attn_prompt",
    )(q, kb, vb, za, lam_blk, sw)


def _ssd_sample_kernel(xbc_ref, dt_ref, z_ref, sc_ref, h0_ref, cw_ref, cb_ref, dtb_ref, alog_ref, dsk_ref,
                       nw_ref, exp_ref, y_ref, cn_ref, h_ref, *, nb):
    x_new = xbc_ref[...]
    c0, c1, c2 = sc_ref[:, 0, :], sc_ref[:, 1, :], sc_ref[:, 2, :]
    acc = cb_ref[...] + cw_ref[0:1, :] * c0
    acc = acc + cw_ref[1:2, :] * c1
    acc = acc + cw_ref[2:3, :] * c2
    acc = acc + cw_ref[3:4, :] * x_new
    cn_ref[:, 0, :] = c1
    cn_ref[:, 1, :] = c2
    cn_ref[:, 2, :] = x_new
    act = _silu(acc)
    xs = act[:, 0:D_SSD]
    gn = SSD_STATE
    b_g = [act[:, D_SSD + g * gn:D_SSD + (g + 1) * gn] for g in range(SSD_GROUPS)]
    c_g = [act[:, D_SSD + (SSD_GROUPS + g) * gn:D_SSD + (SSD_GROUPS + g + 1) * gn] for g in range(SSD_GROUPS)]

    dt = _softplus(dt_ref[...] + dtb_ref[...])
    d_a = dt * (-jnp.exp(alog_ref[...]))
    e_mat = exp_ref[...]
    dt_e = _dot_sel_rhs(dt, e_mat)
    dec_e = _dot_sel_rhs(jnp.exp(d_a), e_mat)
    xdt = xs * dt_e

    heads_per_group = SSD_HEADS // SSD_GROUPS
    gw = heads_per_group * SSD_HEAD_DIM
    pad = jnp.zeros((LANES - nb, D_SSD), F32)
    xdt_t = jnp.concatenate([xdt, pad], axis=0).T
    dec_t = jnp.concatenate([dec_e, pad], axis=0).T
    rowid = lax.broadcasted_iota(jnp.int32, (nb, gw), 0)
    y_parts = []
    for g in range(SSD_GROUPS):
        cbf = c_g[g].astype(BF16)
        cbdot = jnp.sum(c_g[g] * b_g[g], axis=-1, keepdims=True)
        y_off = jnp.zeros((nb, gw), F32)
        for j in range(nb):
            h_old = h0_ref[j, g * gw:(g + 1) * gw, :]
            yo = _dot_nt(cbf, h_old.astype(BF16))
            y_off = jnp.where(rowid == j, yo, y_off)
            xcol = xdt_t[g * gw:(g + 1) * gw, j:j + 1]
            dcol = dec_t[g * gw:(g + 1) * gw, j:j + 1]
            h_ref[j, g * gw:(g + 1) * gw, :] = dcol * h_old + xcol * b_g[g][j:j + 1, :]
        sl = slice(g * gw, (g + 1) * gw)
        y_parts.append((cbdot * dt_e[:, sl]) * xs[:, sl] + y_off * dec_e[:, sl])
    y = jnp.concatenate(y_parts, axis=1) + dsk_ref[...] * xs
    y = y * _silu(z_ref[...])
    outs = []
    for g in range(SSD_GROUPS):
        yg = y[:, g * gw:(g + 1) * gw]
        ms = jnp.mean(jnp.square(yg), axis=-1, keepdims=True)
        outs.append(yg * lax.rsqrt(ms + NORM_EPS) * nw_ref[:, g * gw:(g + 1) * gw])
    y_ref[...] = jnp.concatenate(outs, axis=1).astype(y_ref.dtype)


def _ssd_sample(xbc, dt, z, state_conv, state_ssm, layer, cw, cb, dtb, alog, dsk, nw, e_mat, nb=8):
    db = xbc.shape[0]
    assert db % nb == 0
    hp = SSD_HEADS * SSD_HEAD_DIM
    ssm = state_ssm.reshape(DEPTH, db, hp, SSD_STATE)
    tok = lambda w: pl.BlockSpec((nb, w), lambda i: (i, 0))
    full = lambda a: pl.BlockSpec(a.shape, lambda i: (0,) * a.ndim)
    return pl.pallas_call(
        functools.partial(_ssd_sample_kernel, nb=nb),
        out_shape=(jax.ShapeDtypeStruct((db, D_SSD), BF16),
                   jax.ShapeDtypeStruct((db, SSD_CONV - 1, CONV_DIM), F32),
                   jax.ShapeDtypeStruct((db, hp, SSD_STATE), F32)),
        grid=(db // nb,),
        in_specs=[tok(CONV_DIM), tok(DT_PAD), tok(D_SSD),
                  pl.BlockSpec((None, nb, SSD_CONV - 1, CONV_DIM), lambda i: (layer, i, 0, 0)),
                  pl.BlockSpec((None, nb, hp, SSD_STATE), lambda i: (layer, i, 0, 0)),
                  full(cw), full(cb), full(dtb), full(alog), full(dsk), full(nw), full(e_mat)],
        out_specs=(tok(D_SSD),
                   pl.BlockSpec((nb, SSD_CONV - 1, CONV_DIM), lambda i: (i, 0, 0)),
                   pl.BlockSpec((nb, hp, SSD_STATE), lambda i: (i, 0, 0))),
        compiler_params=_cparams(("arbitrary",)),
        name="ssd_sample",
    )(xbc, dt, z, state_conv, ssm, cw, cb, dtb, alog, dsk, nw, e_mat)


N_QROWS = 2 * ATT_HEADS


def _attn_sample_kernel(pt_ref, q_ref, kn_ref, vn_ref, za_ref, lam_ref, sw_ref, *rest, pp):
    k_refs = rest[0:pp]
    v_refs = rest[pp:2 * pp]
    o_ref = rest[2 * pp]
    qsel_ref, m_ref, l_ref, acc_ref = rest[2 * pp + 1:]
    j = pl.program_id(1)
    nj = pl.num_programs(1)
    rowi = lax.broadcasted_iota(jnp.int32, (N_QROWS, QK_DIM), 0)
    coli = lax.broadcasted_iota(jnp.int32, (N_QROWS, QK_DIM), 1)
    head = rowi & (ATT_HEADS - 1)
    own = (coli >> 6) == (2 * head + (rowi >> 3))
    vown = (coli >> 7) == head

    @pl.when(j == 0)
    def _():
        qb = jnp.broadcast_to(q_ref[0].astype(F32), (N_QROWS, QK_DIM))
        qsel_ref[...] = jnp.where(own, qb, 0.0).astype(BF16)
        m_ref[...] = jnp.full(m_ref.shape, -jnp.inf, F32)
        l_ref[...] = jnp.zeros(l_ref.shape, F32)
        acc_ref[...] = jnp.zeros(acc_ref.shape, F32)

    qsel = qsel_ref[...]
    s = jnp.concatenate([_dot_nt(qsel, k_refs[i][...].astype(BF16)) for i in range(pp)], axis=1)
    m_prev = m_ref[...]
    m_new = jnp.maximum(m_prev, jnp.max(s, axis=-1, keepdims=True))
    alpha = jnp.exp(m_prev - m_new)
    p = jnp.exp(s - m_new)
    l_ref[...] = alpha * l_ref[...] + jnp.sum(p, axis=-1, keepdims=True)
    pb = p.astype(BF16)
    pv = _dot(pb[:, 0:LANES], v_refs[0][...].astype(BF16))
    for i in range(1, pp):
        pv = pv + _dot(pb[:, i * LANES:(i + 1) * LANES], v_refs[i][...].astype(BF16))
    acc_ref[...] = alpha * acc_ref[...] + pv
    m_ref[...] = m_new

    @pl.when(j == nj - 1)
    def _():
        kn = jnp.broadcast_to(kn_ref[0], (N_QROWS, QK_DIM))
        s_new = jnp.sum(qsel_ref[...].astype(F32) * kn, axis=-1, keepdims=True)
        m_p = m_ref[...]
        m_f = jnp.maximum(m_p, s_new)
        a_f = jnp.exp(m_p - m_f)
        p_new = jnp.exp(s_new - m_f)
        l_f = a_f * l_ref[...] + p_new
        vn = jnp.broadcast_to(vn_ref[0], (N_QROWS, D_ATT))
        acc = a_f * acc_ref[...] + p_new * vn
        o = acc / l_f
        lam, lam_init = _lambda_from(lam_ref)
        od = o[0:ATT_HEADS, :] - lam * o[ATT_HEADS:N_QROWS, :]
        od = jnp.where(vown[0:ATT_HEADS, :], od, 0.0)
        ms = jnp.sum(jnp.square(od), axis=-1, keepdims=True) / ATT_V_DIM
        on = od * lax.rsqrt(ms + NORM_EPS)
        orow = jnp.sum(on, axis=0, keepdims=True)
        orow = orow * sw_ref[...] * (1.0 - lam_init)
        o_ref[0] = (orow * _silu(za_ref[0])).astype(o_ref.dtype)


def _attn_sample(page_table, q, k_new, v_new, za, cache_k, cache_v, layer, lam_blk, sw_t, pp=8):
    db, n_pages = page_table.shape
    assert n_pages % pp == 0
    n_pool, page = cache_k.shape[1], cache_k.shape[2]
    ck = cache_k.reshape(DEPTH, n_pool, page, QK_DIM)
    cv = cache_v.reshape(DEPTH, n_pool, page, D_ATT)
    tok = lambda w: pl.BlockSpec((1, 1, w), lambda b, j, pt: (b, 0, 0))
    full = lambda a: pl.BlockSpec(a.shape, lambda b, j, pt: (0, 0))

    def page_spec(i, w):
        return pl.BlockSpec((None, None, page, w), lambda b, j, pt: (layer, pt[b, j * pp + i], 0, 0))

    grid_spec = pltpu.PrefetchScalarGridSpec(
        num_scalar_prefetch=1,
        grid=(db, n_pages // pp),
        in_specs=[tok(QK_DIM), tok(QK_DIM), tok(D_ATT), tok(D_ATT), full(lam_blk), full(sw_t)]
        + [page_spec(i, QK_DIM) for i in range(pp)] + [page_spec(i, D_ATT) for i in range(pp)],
        out_specs=tok(D_ATT),
        scratch_shapes=[pltpu.VMEM((N_QROWS, QK_DIM), BF16), pltpu.VMEM((N_QROWS, 1), F32),
                        pltpu.VMEM((N_QROWS, 1), F32), pltpu.VMEM((N_QROWS, D_ATT), F32)],
    )
    r3 = lambda a: a.reshape(db, 1, a.shape[-1])
    out = pl.pallas_call(
        functools.partial(_attn_sample_kernel, pp=pp),
        out_shape=jax.ShapeDtypeStruct((db, 1, D_ATT), BF16),
        grid_spec=grid_spec,
        compiler_params=_cparams(("arbitrary", "arbitrary")),
        name="attn_sample",
    )(page_table, r3(q), r3(k_new), r3(v_new), r3(za), lam_blk, sw_t, *([ck] * pp), *([cv] * pp))
    return out.reshape(db, D_ATT)


def _pad_lanes(v, width=LANES):
    v = v.reshape(1, -1)
    return jnp.pad(v, ((0, 0), (0, width - v.shape[1])))


def _pick_tile(m, pref):
    t = min(pref, m)
    while m % t:
        t //= 2
    return t


def kernel(x_prompt, x_sample, cache_k, cache_v, state_ssm, state_conv, page_table, p_prompt, p_sample,
           w_in, conv_w, conv_b, dt_bias, a_log, d_skip, ssd_norm_w, lam_q1, lam_k1, lam_q2, lam_k2,
           subln_w, w_out, ln_g, ln_b, w_ple, w_ple_gate, b_ple_gate):
    b, l, _ = x_prompt.shape
    db = x_sample.shape[0]
    mp = b * l
    xp = x_prompt.reshape(mp, D_MODEL)
    xs = x_sample.reshape(db, D_MODEL)
    e_np = _head_expand_matrix()
    e_mat = jnp.asarray(e_np, BF16)
    e_mat_t = jnp.asarray(e_np.T, BF16)
    tm_in = _pick_tile(mp, 256)
    tm_out = _pick_tile(mp, 512)
    tq = _pick_tile(l, 512)
    row = lambda v: v.reshape(1, -1)

    kp, vp, hp, cp, ksl, vsl, hsl, csl = [], [], [], [], [], [], [], []
    for layer in range(DEPTH):
        lam_init = 0.8 - 0.6 * math.exp(-0.3 * layer)
        wl = w_in[layer]
        dt_lo = D_SSD + CONV_DIM
        w_r = jnp.concatenate(
            [wl[:, :dt_lo], wl[:, dt_lo + SSD_HEADS:], wl[:, dt_lo:dt_lo + SSD_HEADS],
             jnp.zeros((D_MODEL, DT_PAD - SSD_HEADS), wl.dtype)], axis=1).astype(BF16)
        cw, cb = conv_w[layer], row(conv_b[layer])
        dtb, alog = _pad_lanes(dt_bias[layer]), _pad_lanes(a_log[layer])
        dsk = row(jnp.repeat(d_skip[layer], SSD_HEAD_DIM))
        nw = row(ssd_norm_w[layer])
        lam_blk = jnp.concatenate(
            [_pad_lanes(lam_q1[layer]), _pad_lanes(lam_k1[layer]), _pad_lanes(lam_q2[layer]),
             _pad_lanes(lam_k2[layer]), jnp.full((1, LANES), lam_init, F32), jnp.zeros((3, LANES), F32)], axis=0)
        sw = row(subln_w[layer])
        sw_t = row(jnp.tile(subln_w[layer], ATT_HEADS))
        wo = w_out[layer].astype(BF16)
        wp = w_ple[layer].astype(BF16)
        wg = w_ple_gate[layer].astype(BF16)
        g, bb, bg = row(ln_g[layer]), row(ln_b[layer]), row(b_ple_gate[layer])

        z, xbc, dt, q, k, v, za, kb, vb = _in_proj(xp, w_r, tm_in)
        r3 = lambda a: a.reshape(b, l, a.shape[-1])
        y_ssd, h_new = _ssd_prompt(r3(xbc), r3(dt), r3(z), cw, cb, dtb, alog, dsk, nw, e_mat, e_mat_t)
        y_att = _attn_prompt(r3(q), r3(kb), r3(vb), r3(za), lam_blk, sw, tq)
        xp = _layer_out(y_ssd.reshape(mp, D_SSD), y_att.reshape(mp, D_ATT), xp,
                        p_prompt[layer].reshape(mp, PLE_DIM), wo, g, bb, wp, wg, bg, tm_out)
        kp.append(k.reshape(b, l, ATT_HEADS, 2, ATT_HEAD_DIM))
        vp.append(v.reshape(b, l, ATT_HEADS, ATT_V_DIM))
        hp.append(h_new.reshape(b, SSD_HEADS, SSD_HEAD_DIM, SSD_STATE))
        cp.append(r3(xbc)[:, l - (SSD_CONV - 1):, :])

        z, xbc, dt, q, k, v, za, _, _ = _in_proj(xs, w_r, db)
        y_ssd, conv_new, h_new = _ssd_sample(xbc, dt, z, state_conv, state_ssm, layer,
                                             cw, cb, dtb, alog, dsk, nw, e_mat)
        y_att = _attn_sample(page_table, q, k, v, za, cache_k, cache_v, layer, lam_blk, sw_t)
        xs = _layer_out(y_ssd, y_att, xs, p_sample[layer].reshape(db, PLE_DIM), wo, g, bb, wp, wg, bg, db)
        ksl.append(k.reshape(db, 1, ATT_HEADS, 2, ATT_HEAD_DIM))
        vsl.append(v.reshape(db, 1, ATT_HEADS, ATT_V_DIM))
        hsl.append(h_new.reshape(db, SSD_HEADS, SSD_HEAD_DIM, SSD_STATE))
        csl.append(conv_new)

    return (xp.reshape(b, l, D_MODEL), xs.reshape(db, 1, D_MODEL), jnp.stack(kp), jnp.stack(vp),
            jnp.stack(hp), jnp.stack(cp), jnp.stack(ksl), jnp.stack(vsl), jnp.stack(hsl), jnp.stack(csl))
```

```python
import functools
import math

import numpy as np
import jax
import jax.numpy as jnp
from jax import lax
from jax.experimental import pallas as pl
from jax.experimental.pallas import tpu as pltpu

F32 = jnp.float32
BF16 = jnp.bfloat16

D_MODEL = 1024
DEPTH = 4
D_SSD = 1024
SSD_HEAD_DIM = 64
SSD_HEADS = 16
SSD_STATE = 128
SSD_GROUPS = 2
SSD_CONV = 4
SSD_CHUNK = 128
CONV_DIM = D_SSD + 2 * SSD_GROUPS * SSD_STATE
ATT_HEAD_DIM = 64
ATT_V_DIM = 128
ATT_HEADS = 8
D_ATT = 1024
QK_DIM = 1024
ATT_SCALE = ATT_HEAD_DIM ** -0.5
Q_SCALE = ATT_SCALE * math.log2(math.e)
PLE_DIM = 256
DEEPNORM_ALPHA = (2 * DEPTH) ** 0.25
NORM_EPS = 1e-5
LANES = 128
DT_PAD = LANES
C_Z, C_XBC, C_Q, C_K, C_V, C_ZA, C_DT, C_END = 0, 1024, 2560, 3584, 4608, 5632, 6656, 6784
VMEM_LIMIT = 56 * 1024 * 1024


def _cparams(sem):
    return pltpu.CompilerParams(dimension_semantics=sem, vmem_limit_bytes=VMEM_LIMIT)


def _silu(x):
    return x * jax.nn.sigmoid(x)


def _softplus(x):
    return jnp.maximum(x, 0.0) + jnp.log1p(jnp.exp(-jnp.abs(x)))


def _split3(a):
    hi = a.astype(BF16)
    r1 = a - hi.astype(F32)
    mid = r1.astype(BF16)
    lo = (r1 - mid.astype(F32)).astype(BF16)
    return hi, mid, lo


def _dot(a, b):
    return jnp.dot(a, b, preferred_element_type=F32)


def _dot_nt(a, b):
    return lax.dot_general(a, b, (((1,), (1,)), ((), ())), preferred_element_type=F32)


def _dot_sel_rhs(a, sel):
    hi, mid, lo = _split3(a)
    return (_dot(hi, sel) + _dot(mid, sel)) + _dot(lo, sel)


def _dot_sel_lhs(sel, b):
    hi, mid, lo = _split3(b)
    return (_dot(sel, hi) + _dot(sel, mid)) + _dot(sel, lo)


def _in_proj_kernel(x_ref, w_ref, z_ref, xbc_ref, dt_ref, q_ref, k_ref, v_ref, za_ref, kb_ref, vb_ref):
    xb = x_ref[...].astype(BF16)

    def mm(lo, hi):
        return _dot(xb, w_ref[:, lo:hi])

    z_ref[...] = mm(C_Z, C_XBC)
    xbc_ref[...] = mm(C_XBC, C_Q)
    q_ref[...] = (mm(C_Q, C_K) * Q_SCALE).astype(BF16)
    k = mm(C_K, C_V)
    k_ref[...] = k
    kb_ref[...] = k.astype(BF16)
    v = mm(C_V, C_ZA)
    v_ref[...] = v
    vb_ref[...] = v.astype(BF16)
    za_ref[...] = mm(C_ZA, C_DT)
    dt_ref[...] = mm(C_DT, C_END)


def _in_proj(x, w_r, tm):
    m = x.shape[0]
    assert m % tm == 0
    widths = (1024, CONV_DIM, DT_PAD, 1024, 1024, 1024, 1024, 1024, 1024)
    dtypes = (F32, F32, F32, BF16, F32, F32, F32, BF16, BF16)
    out_shape = tuple(jax.ShapeDtypeStruct((m, w), d) for w, d in zip(widths, dtypes))
    out_specs = tuple(pl.BlockSpec((tm, w), lambda i: (i, 0)) for w in widths)
    return pl.pallas_call(
        _in_proj_kernel,
        out_shape=out_shape,
        grid=(m // tm,),
        in_specs=[pl.BlockSpec((tm, D_MODEL), lambda i: (i, 0)),
                  pl.BlockSpec((D_MODEL, C_END), lambda i: (0, 0))],
        out_specs=out_specs,
        compiler_params=_cparams(("arbitrary",)),
        name="in_proj",
    )(x, w_r)


def _layer_out_kernel(ys_ref, ya_ref, x_ref, p_ref, wo_ref, g_ref, b_ref, wp_ref, wg_ref, bg_ref, o_ref):
    y = _dot(ys_ref[...].astype(BF16), wo_ref[0:D_SSD, :]) + _dot(ya_ref[...].astype(BF16), wo_ref[D_SSD:, :])
    t = DEEPNORM_ALPHA * x_ref[...] + y
    mu = jnp.mean(t, axis=-1, keepdims=True)
    var = jnp.mean(jnp.square(t - mu), axis=-1, keepdims=True)
    h = ((t - mu) * lax.rsqrt(var + NORM_EPS)) * g_ref[...] + b_ref[...]
    gate = jax.nn.sigmoid(_dot(h.astype(BF16), wg_ref[...]) + bg_ref[...])
    o_ref[...] = h + gate * _dot(p_ref[...].astype(BF16), wp_ref[...])


def _layer_out(ys, ya, x, p, wo, g, b, wp, wg, bg, tm):
    m = x.shape[0]
    assert m % tm == 0
    row = lambda w: pl.BlockSpec((tm, w), lambda i: (i, 0))
    full = lambda a: pl.BlockSpec(a.shape, lambda i: (0, 0))
    return pl.pallas_call(
        _layer_out_kernel,
        out_shape=jax.ShapeDtypeStruct((m, D_MODEL), F32),
        grid=(m // tm,),
        in_specs=[row(D_SSD), row(D_ATT), row(D_MODEL), row(PLE_DIM),
                  full(wo), full(g), full(b), full(wp), full(wg), full(bg)],
        out_specs=row(D_MODEL),
        compiler_params=_cparams(("arbitrary",)),
        name="layer_out",
    )(ys, ya, x, p, wo, g, b, wp, wg, bg)


def _head_expand_matrix():
    e = np.zeros((LANES, D_SSD), np.float32)
    for h in range(SSD_HEADS):
        e[h, h * SSD_HEAD_DIM:(h + 1) * SSD_HEAD_DIM] = 1.0
    return e


def _ssd_prompt_kernel(xbc_ref, dt_ref, z_ref, cw_ref, cb_ref, dtb_ref, alog_ref, dsk_ref, nw_ref,
                       exp_ref, expt_ref, y_ref, h_ref, xpad_ref):
    cl = SSD_CHUNK
    c = pl.program_id(1)

    @pl.when(c == 0)
    def _():
        xpad_ref[0:8, :] = jnp.zeros((8, CONV_DIM), F32)
        h_ref[...] = jnp.zeros(h_ref.shape, F32)

    xpad_ref[8:8 + cl, :] = xbc_ref[0]
    acc = cb_ref[...] + cw_ref[0:1, :] * xpad_ref[5:5 + cl, :]
    acc = acc + cw_ref[1:2, :] * xpad_ref[6:6 + cl, :]
    acc = acc + cw_ref[2:3, :] * xpad_ref[7:7 + cl, :]
    acc = acc + cw_ref[3:4, :] * xpad_ref[8:8 + cl, :]
    xpad_ref[5:8, :] = xpad_ref[5 + cl:8 + cl, :]
    act = _silu(acc)
    xs = act[:, 0:D_SSD]
    gn = SSD_STATE
    b_g = [act[:, D_SSD + g * gn:D_SSD + (g + 1) * gn] for g in range(SSD_GROUPS)]
    c_g = [act[:, D_SSD + (SSD_GROUPS + g) * gn:D_SSD + (SSD_GROUPS + g + 1) * gn] for g in range(SSD_GROUPS)]

    dt = _softplus(dt_ref[0] + dtb_ref[...])
    a_neg = -jnp.exp(alog_ref[...])
    d_a = dt * a_neg
    row = lax.broadcasted_iota(jnp.int32, (cl, cl), 0)
    col = lax.broadcasted_iota(jnp.int32, (cl, cl), 1)
    causal = col <= row
    tril = jnp.where(causal, 1.0, 0.0).astype(BF16)
    acum = _dot_sel_lhs(tril, d_a)
    acum_t = acum.T
    dt_t = dt.T
    a_last = acum[cl - 1:cl, :]

    e_mat = exp_ref[...]
    e_acum = _dot_sel_rhs(jnp.exp(acum), e_mat)
    e_rest = _dot_sel_rhs(jnp.exp(a_last - acum) * dt, e_mat)
    xw = xs * e_rest
    cd_col = jnp.broadcast_to(jnp.exp(acum_t[:, cl - 1:cl]), (LANES, LANES))
    cd_full = _dot_sel_lhs(expt_ref[...], cd_col)

    lane = lax.broadcasted_iota(jnp.int32, (cl, LANES), 1)
    lo_half = lane < SSD_HEAD_DIM
    heads_per_group = SSD_HEADS // SSD_GROUPS
    gw = heads_per_group * SSD_HEAD_DIM
    y_parts = []
    for g in range(SSD_GROUPS):
        cbf = c_g[g].astype(BF16)
        bbf = b_g[g].astype(BF16)
        cb = _dot_nt(cbf, bbf)
        h_old = h_ref[0, g * gw:(g + 1) * gw, :]
        y_off = _dot_nt(cbf, h_old.astype(BF16))
        for hp in range(heads_per_group // 2):
            x_pair = xs[:, g * gw + hp * LANES:g * gw + (hp + 1) * LANES]
            pair = None
            for s in range(2):
                h = g * heads_per_group + 2 * hp + s
                seg = acum[:, h:h + 1] - acum_t[h:h + 1, :]
                decay = jnp.exp(jnp.where(causal, seg, -jnp.inf))
                w = (cb * decay) * dt_t[h:h + 1, :]
                x_half = jnp.where(lo_half if s == 0 else jnp.logical_not(lo_half), x_pair, 0.0)
                yd = _dot(w.astype(BF16), x_half.astype(BF16))
                pair = yd if pair is None else pair + yd
            y_parts.append(pair + y_off[:, hp * LANES:(hp + 1) * LANES]
                           * e_acum[:, g * gw + hp * LANES:g * gw + (hp + 1) * LANES])
        st = _dot(xw[:, g * gw:(g + 1) * gw].T.astype(BF16), bbf)
        h_ref[0, g * gw:(g + 1) * gw, :] = cd_full[g * gw:(g + 1) * gw, :] * h_old + st

    y = jnp.concatenate(y_parts, axis=1) + dsk_ref[...] * xs
    y = y * _silu(z_ref[0])
    outs = []
    for g in range(SSD_GROUPS):
        yg = y[:, g * gw:(g + 1) * gw]
        ms = jnp.mean(jnp.square(yg), axis=-1, keepdims=True)
        outs.append(yg * lax.rsqrt(ms + NORM_EPS) * nw_ref[:, g * gw:(g + 1) * gw])
    y_ref[0] = jnp.concatenate(outs, axis=1).astype(y_ref.dtype)


def _ssd_prompt(xbc, dt, z, cw, cb, dtb, alog, dsk, nw, e_mat, e_mat_t):
    b, l, _ = xbc.shape
    assert l % SSD_CHUNK == 0
    nc = l // SSD_CHUNK
    tok = lambda w: pl.BlockSpec((1, SSD_CHUNK, w), lambda i, j: (i, j, 0))
    full = lambda a: pl.BlockSpec(a.shape, lambda i, j: (0,) * a.ndim)
    return pl.pallas_call(
        _ssd_prompt_kernel,
        out_shape=(jax.ShapeDtypeStruct((b, l, D_SSD), BF16),
                   jax.ShapeDtypeStruct((b, SSD_HEADS * SSD_HEAD_DIM, SSD_STATE), F32)),
        grid=(b, nc),
        in_specs=[tok(CONV_DIM), tok(DT_PAD), tok(D_SSD), full(cw), full(cb), full(dtb), full(alog),
                  full(dsk), full(nw), full(e_mat), full(e_mat_t)],
        out_specs=(tok(D_SSD),
                   pl.BlockSpec((1, SSD_HEADS * SSD_HEAD_DIM, SSD_STATE), lambda i, j: (i, 0, 0))),
        scratch_shapes=[pltpu.VMEM((8 + SSD_CHUNK, CONV_DIM), F32)],
        compiler_params=_cparams(("arbitrary", "arbitrary")),
        name="ssd_prompt",
    )(xbc, dt, z, cw, cb, dtb, alog, dsk, nw, e_mat, e_mat_t)


def _lambda_from(lam_ref):
    s1 = jnp.sum(lam_ref[0:1, :] * lam_ref[1:2, :], axis=-1, keepdims=True)
    s2 = jnp.sum(lam_ref[2:3, :] * lam_ref[3:4, :], axis=-1, keepdims=True)
    lam_init = lam_ref[4:5, 0:1]
    return jnp.exp(s1) - jnp.exp(s2) + lam_init, lam_init


def _attn_prompt_kernel(q_ref, k_ref, v_ref, za_ref, lam_ref, sw_ref, o_ref, qst_ref, vt_ref, m_ref, l_ref, acc_ref,
                        *, tq, tk, rs):
    qi = pl.program_id(2)
    nkb = vt_ref.shape[0]

    @pl.when(qi == 0)
    def _():
        for c in range(nkb):
            vt_ref[c] = v_ref[0, c * tk:(c + 1) * tk, :].astype(F32).T.astype(BF16)

    qt = q_ref[0].astype(F32).T
    rowd = lax.broadcasted_iota(jnp.int32, (LANES, tq), 0)
    qst_ref[:, 0:tq] = jnp.where(rowd < ATT_HEAD_DIM, qt, 0.0).astype(BF16)
    qst_ref[:, tq:2 * tq] = jnp.where(rowd >= ATT_HEAD_DIM, qt, 0.0).astype(BF16)
    m_ref[...] = jnp.full(m_ref.shape, -jnp.inf, F32)
    l_ref[...] = jnp.zeros(l_ref.shape, F32)
    acc_ref[...] = jnp.zeros(acc_ref.shape, F32)
    n_strips = (2 * tq) // rs

    def block(ki, diag_off):
        start = pl.multiple_of(ki * tk, tk)
        k = k_ref[0, pl.ds(start, tk), :]
        vt = vt_ref[ki]
        strips = []
        for r in range(n_strips):
            q_lo = (r * rs) % tq
            if diag_off is not None and diag_off > q_lo + rs - 1:
                continue
            strips.append((slice(r * rs, (r + 1) * rs), q_lo))
        scores = []
        for sl, q_lo in strips:
            s = _dot(k, qst_ref[:, sl])
            if diag_off is not None:
                kk = lax.broadcasted_iota(jnp.int32, (tk, rs), 0) + diag_off
                qq = lax.broadcasted_iota(jnp.int32, (tk, rs), 1) + q_lo
                s = jnp.where(kk <= qq, s, -jnp.inf)
            scores.append(s)
        probs = []
        for (sl, _), s in zip(strips, scores):
            m_prev = m_ref[:, sl]
            m_new = jnp.maximum(m_prev, jnp.max(s, axis=0, keepdims=True))
            alpha = jnp.exp2(m_prev - m_new)
            p = jnp.exp2(s - m_new)
            l_ref[:, sl] = alpha * l_ref[:, sl] + jnp.sum(p, axis=0, keepdims=True)
            m_ref[:, sl] = m_new
            probs.append((alpha, p.astype(BF16)))
        for (sl, _), (alpha, pb) in zip(strips, probs):
            acc_ref[:, sl] = alpha * acc_ref[:, sl] + _dot(vt, pb)

    def body(ki, carry):
        block(ki, None)
        return carry

    blocks_per_q = tq // tk
    lax.fori_loop(0, qi * blocks_per_q, body, 0)
    for j in range(blocks_per_q):
        block(qi * blocks_per_q + j, j * tk)

    ot = acc_ref[...] / l_ref[...]
    lam, lam_init = _lambda_from(lam_ref)
    od = (ot[:, 0:tq] - lam * ot[:, tq:2 * tq]).T
    ms = jnp.mean(jnp.square(od), axis=-1, keepdims=True)
    on = od * lax.rsqrt(ms + NORM_EPS) * sw_ref[...] * (1.0 - lam_init)
    o_ref[0] = (on * _silu(za_ref[0])).astype(o_ref.dtype)


def _attn_prompt(q, kb, vb, za, lam_blk, sw, tq, tk, rs):
    b, l, _ = q.shape
    assert l % tq == 0 and tq % tk == 0 and (2 * tq) % rs == 0 and tq % rs == 0
    nq = l // tq
    qspec = pl.BlockSpec((1, tq, LANES), lambda bi, h, i: (bi, i, h))
    kspec = pl.BlockSpec((1, l, LANES), lambda bi, h, i: (bi, 0, h))
    full = lambda a: pl.BlockSpec(a.shape, lambda bi, h, i: (0, 0))
    return pl.pallas_call(
        functools.partial(_attn_prompt_kernel, tq=tq, tk=tk, rs=rs),
        out_shape=jax.ShapeDtypeStruct((b, l, D_ATT), BF16),
        grid=(b, ATT_HEADS, nq),
        in_specs=[qspec, kspec, kspec, qspec, full(lam_blk), full(sw)],
        out_specs=qspec,
        scratch_shapes=[pltpu.VMEM((LANES, 2 * tq), BF16), pltpu.VMEM((l // tk, ATT_V_DIM, tk), BF16),
                        pltpu.VMEM((1, 2 * tq), F32), pltpu.VMEM((1, 2 * tq), F32),
                        pltpu.VMEM((ATT_V_DIM, 2 * tq), F32)],
        compiler_params=_cparams(("arbitrary", "arbitrary", "arbitrary")),
        name="attn_prompt",
    )(q, kb, vb, za, lam_blk, sw)


def _ssd_sample_kernel(xbc_ref, dt_ref, z_ref, sc_ref, h0_ref, cw_ref, cb_ref, dtb_ref, alog_ref, dsk_ref,
                       nw_ref, exp_ref, y_ref, cn_ref, h_ref, *, nb):
    x_new = xbc_ref[...]
    c0, c1, c2 = sc_ref[:, 0, :], sc_ref[:, 1, :], sc_ref[:, 2, :]
    acc = cb_ref[...] + cw_ref[0:1, :] * c0
    acc = acc + cw_ref[1:2, :] * c1
    acc = acc + cw_ref[2:3, :] * c2
    acc = acc + cw_ref[3:4, :] * x_new
    cn_ref[:, 0, :] = c1
    cn_ref[:, 1, :] = c2
    cn_ref[:, 2, :] = x_new
    act = _silu(acc)
    xs = act[:, 0:D_SSD]
    gn = SSD_STATE
    b_g = [act[:, D_SSD + g * gn:D_SSD + (g + 1) * gn] for g in range(SSD_GROUPS)]
    c_g = [act[:, D_SSD + (SSD_GROUPS + g) * gn:D_SSD + (SSD_GROUPS + g + 1) * gn] for g in range(SSD_GROUPS)]

    dt = _softplus(dt_ref[...] + dtb_ref[...])
    d_a = dt * (-jnp.exp(alog_ref[...]))
    e_mat = exp_ref[...]
    dt_e = _dot_sel_rhs(dt, e_mat)
    dec_e = _dot_sel_rhs(jnp.exp(d_a), e_mat)
    xdt = xs * dt_e

    heads_per_group = SSD_HEADS // SSD_GROUPS
    gw = heads_per_group * SSD_HEAD_DIM
    pad = jnp.zeros((LANES - nb, D_SSD), F32)
    xdt_t = jnp.concatenate([xdt, pad], axis=0).T
    dec_t = jnp.concatenate([dec_e, pad], axis=0).T
    rowid = lax.broadcasted_iota(jnp.int32, (nb, gw), 0)
    y_parts = []
    for g in range(SSD_GROUPS):
        cbf = c_g[g].astype(BF16)
        cbdot = jnp.sum(c_g[g] * b_g[g], axis=-1, keepdims=True)
        y_off = jnp.zeros((nb, gw), F32)
        for j in range(nb):
            h_old = h0_ref[j, g * gw:(g + 1) * gw, :]
            yo = _dot_nt(cbf, h_old.astype(BF16))
            y_off = jnp.where(rowid == j, yo, y_off)
            xcol = xdt_t[g * gw:(g + 1) * gw, j:j + 1]
            dcol = dec_t[g * gw:(g + 1) * gw, j:j + 1]
            h_ref[j, g * gw:(g + 1) * gw, :] = dcol * h_old + xcol * b_g[g][j:j + 1, :]
        sl = slice(g * gw, (g + 1) * gw)
        y_parts.append((cbdot * dt_e[:, sl]) * xs[:, sl] + y_off * dec_e[:, sl])
    y = jnp.concatenate(y_parts, axis=1) + dsk_ref[...] * xs
    y = y * _silu(z_ref[...])
    outs = []
    for g in range(SSD_GROUPS):
        yg = y[:, g * gw:(g + 1) * gw]
        ms = jnp.mean(jnp.square(yg), axis=-1, keepdims=True)
        outs.append(yg * lax.rsqrt(ms + NORM_EPS) * nw_ref[:, g * gw:(g + 1) * gw])
    y_ref[...] = jnp.concatenate(outs, axis=1).astype(y_ref.dtype)


def _ssd_sample(xbc, dt, z, state_conv, state_ssm, layer, cw, cb, dtb, alog, dsk, nw, e_mat, nb=8):
    db = xbc.shape[0]
    assert db % nb == 0
    hp = SSD_HEADS * SSD_HEAD_DIM
    ssm = state_ssm.reshape(DEPTH, db, hp, SSD_STATE)
    tok = lambda w: pl.BlockSpec((nb, w), lambda i: (i, 0))
    full = lambda a: pl.BlockSpec(a.shape, lambda i: (0,) * a.ndim)
    return pl.pallas_call(
        functools.partial(_ssd_sample_kernel, nb=nb),
        out_shape=(jax.ShapeDtypeStruct((db, D_SSD), BF16),
                   jax.ShapeDtypeStruct((db, SSD_CONV - 1, CONV_DIM), F32),
                   jax.ShapeDtypeStruct((db, hp, SSD_STATE), F32)),
        grid=(db // nb,),
        in_specs=[tok(CONV_DIM), tok(DT_PAD), tok(D_SSD),
                  pl.BlockSpec((None, nb, SSD_CONV - 1, CONV_DIM), lambda i: (layer, i, 0, 0)),
                  pl.BlockSpec((None, nb, hp, SSD_STATE), lambda i: (layer, i, 0, 0)),
                  full(cw), full(cb), full(dtb), full(alog), full(dsk), full(nw), full(e_mat)],
        out_specs=(tok(D_SSD),
                   pl.BlockSpec((nb, SSD_CONV - 1, CONV_DIM), lambda i: (i, 0, 0)),
                   pl.BlockSpec((nb, hp, SSD_STATE), lambda i: (i, 0, 0))),
        compiler_params=_cparams(("arbitrary",)),
        name="ssd_sample",
    )(xbc, dt, z, state_conv, ssm, cw, cb, dtb, alog, dsk, nw, e_mat)


N_QROWS = 2 * ATT_HEADS


def _attn_sample_kernel(pt_ref, q_ref, kn_ref, vn_ref, za_ref, lam_ref, sw_ref, rexp_ref, *rest, pp):
    k_refs = rest[0:pp]
    v_refs = rest[pp:2 * pp]
    o_ref = rest[2 * pp]
    qsel_ref, m_ref, l_ref, acc_ref = rest[2 * pp + 1:]
    j = pl.program_id(1)
    nj = pl.num_programs(1)

    @pl.when(j == 0)
    def _():
        rowi = lax.broadcasted_iota(jnp.int32, (N_QROWS, QK_DIM), 0)
        coli = lax.broadcasted_iota(jnp.int32, (N_QROWS, QK_DIM), 1)
        own = (coli >> 6) == (2 * (rowi & (ATT_HEADS - 1)) + (rowi >> 3))
        qb = jnp.broadcast_to(q_ref[0].astype(F32), (N_QROWS, QK_DIM))
        qsel_ref[...] = jnp.where(own, qb, 0.0).astype(BF16)
        m_ref[...] = jnp.full(m_ref.shape, -jnp.inf, F32)
        l_ref[...] = jnp.zeros(l_ref.shape, F32)
        acc_ref[...] = jnp.zeros(acc_ref.shape, F32)

    qsel = qsel_ref[...]
    page = k_refs[0].shape[1]
    s = jnp.concatenate([_dot(qsel, k_refs[i][...].astype(BF16)) for i in range(pp)], axis=1)
    m_prev = m_ref[...]
    m_new = jnp.maximum(m_prev, jnp.max(s, axis=-1, keepdims=True))
    alpha = jnp.exp2(m_prev - m_new)
    p = jnp.exp2(s - m_new)
    l_ref[...] = alpha * l_ref[...] + jnp.sum(p, axis=-1, keepdims=True)
    pb = p.astype(BF16)
    p_stack = jnp.concatenate([pb[:, i * page:(i + 1) * page] for i in range(pp)], axis=0)
    p_exp = _dot(p_stack, rexp_ref[...])
    rowe = lax.broadcasted_iota(jnp.int32, p_exp.shape, 0)
    cole = lax.broadcasted_iota(jnp.int32, p_exp.shape, 1)
    keep = (cole & (ATT_HEADS - 1)) == (rowe & (ATT_HEADS - 1))
    p2 = jnp.where(keep, p_exp, 0.0).astype(BF16)
    pv = _dot(p2[0:N_QROWS, :], v_refs[0][...].astype(BF16))
    for i in range(1, pp):
        pv = pv + _dot(p2[i * N_QROWS:(i + 1) * N_QROWS, :], v_refs[i][...].astype(BF16))
    acc_ref[...] = alpha * acc_ref[...] + pv
    m_ref[...] = m_new

    @pl.when(j == nj - 1)
    def _():
        kn = jnp.broadcast_to(kn_ref[0], (N_QROWS, QK_DIM))
        s_new = jnp.sum(qsel_ref[...].astype(F32) * kn, axis=-1, keepdims=True)
        m_p = m_ref[...]
        m_f = jnp.maximum(m_p, s_new)
        a_f = jnp.exp2(m_p - m_f)
        p_new = jnp.exp2(s_new - m_f)
        l_f = a_f * l_ref[...] + p_new
        vn = jnp.concatenate([vn_ref[0], vn_ref[0]], axis=0)
        o = (a_f * acc_ref[...] + p_new * vn) / l_f
        lam, lam_init = _lambda_from(lam_ref)
        od = o[0:ATT_HEADS, :] - lam * o[ATT_HEADS:N_QROWS, :]
        ms = jnp.mean(jnp.square(od), axis=-1, keepdims=True)
        on = od * lax.rsqrt(ms + NORM_EPS) * sw_ref[...] * (1.0 - lam_init)
        o_ref[0] = on * _silu(za_ref[0])


def _attn_sample(page_table, q, k_new, v_new, za, cache_k, cache_v, layer, lam_blk, sw, pp=8):
    db, n_pages = page_table.shape
    assert n_pages % pp == 0
    n_pool, page = cache_k.shape[1], cache_k.shape[2]
    r_np = np.zeros((page, page * ATT_HEADS), np.float32)
    for key in range(page):
        r_np[key, key * ATT_HEADS:(key + 1) * ATT_HEADS] = 1.0
    rexp = jnp.asarray(r_np, BF16)
    ck = jnp.transpose(cache_k, (0, 1, 3, 4, 5, 2)).reshape(DEPTH, n_pool, QK_DIM, page)
    cv = cache_v.reshape(DEPTH, n_pool, page * ATT_HEADS, ATT_V_DIM)
    tok = lambda w: pl.BlockSpec((1, 1, w), lambda b, j, pt: (b, 0, 0))
    tokh = pl.BlockSpec((1, ATT_HEADS, ATT_V_DIM), lambda b, j, pt: (b, 0, 0))
    full = lambda a: pl.BlockSpec(a.shape, lambda b, j, pt: (0, 0))

    def page_spec(i, rows, w):
        return pl.BlockSpec((None, None, rows, w), lambda b, j, pt: (layer, pt[b, j * pp + i], 0, 0))

    grid_spec = pltpu.PrefetchScalarGridSpec(
        num_scalar_prefetch=1,
        grid=(db, n_pages // pp),
        in_specs=[tok(QK_DIM), tok(QK_DIM), tokh, tokh, full(lam_blk), full(sw), full(rexp)]
        + [page_spec(i, QK_DIM, page) for i in range(pp)]
        + [page_spec(i, page * ATT_HEADS, ATT_V_DIM) for i in range(pp)],
        out_specs=tokh,
        scratch_shapes=[pltpu.VMEM((N_QROWS, QK_DIM), BF16), pltpu.VMEM((N_QROWS, 1), F32),
                        pltpu.VMEM((N_QROWS, 1), F32), pltpu.VMEM((N_QROWS, ATT_V_DIM), F32)],
    )
    r3 = lambda a: a.reshape(db, 1, a.shape[-1])
    rh = lambda a: a.reshape(db, ATT_HEADS, ATT_V_DIM)
    out = pl.pallas_call(
        functools.partial(_attn_sample_kernel, pp=pp),
        out_shape=jax.ShapeDtypeStruct((db, ATT_HEADS, ATT_V_DIM), F32),
        grid_spec=grid_spec,
        compiler_params=_cparams(("arbitrary", "arbitrary")),
        name="attn_sample",
    )(page_table, r3(q), r3(k_new), rh(v_new), rh(za), lam_blk, sw, rexp, *([ck] * pp), *([cv] * pp))
    return out.reshape(db, D_ATT)


def _pad_lanes(v, width=LANES):
    v = v.reshape(1, -1)
    return jnp.pad(v, ((0, 0), (0, width - v.shape[1])))


def _pick_tile(m, pref):
    t = min(pref, m)
    while m % t:
        t //= 2
    return t


def kernel(x_prompt, x_sample, cache_k, cache_v, state_ssm, state_conv, page_table, p_prompt, p_sample,
           w_in, conv_w, conv_b, dt_bias, a_log, d_skip, ssd_norm_w, lam_q1, lam_k1, lam_q2, lam_k2,
           subln_w, w_out, ln_g, ln_b, w_ple, w_ple_gate, b_ple_gate):
    b, l, _ = x_prompt.shape
    db = x_sample.shape[0]
    mp = b * l
    xp = x_prompt.reshape(mp, D_MODEL)
    xs = x_sample.reshape(db, D_MODEL)
    e_np = _head_expand_matrix()
    e_mat = jnp.asarray(e_np, BF16)
    e_mat_t = jnp.asarray(e_np.T, BF16)
    tm_in = _pick_tile(mp, 256)
    tm_out = _pick_tile(mp, 512)
    tq = _pick_tile(l, 1024)
    tk = _pick_tile(tq, 512)
    rs = _pick_tile(tq, 256)
    row = lambda v: v.reshape(1, -1)

    kp, vp, hp, cp, ksl, vsl, hsl, csl = [], [], [], [], [], [], [], []
    for layer in range(DEPTH):
        lam_init = 0.8 - 0.6 * math.exp(-0.3 * layer)
        wl = w_in[layer]
        dt_lo = D_SSD + CONV_DIM
        w_r = jnp.concatenate(
            [wl[:, :dt_lo], wl[:, dt_lo + SSD_HEADS:], wl[:, dt_lo:dt_lo + SSD_HEADS],
             jnp.zeros((D_MODEL, DT_PAD - SSD_HEADS), wl.dtype)], axis=1).astype(BF16)
        cw, cb = conv_w[layer], row(conv_b[layer])
        dtb, alog = _pad_lanes(dt_bias[layer]), _pad_lanes(a_log[layer])
        dsk = row(jnp.repeat(d_skip[layer], SSD_HEAD_DIM))
        nw = row(ssd_norm_w[layer])
        lam_blk = jnp.concatenate(
            [_pad_lanes(lam_q1[layer]), _pad_lanes(lam_k1[layer]), _pad_lanes(lam_q2[layer]),
             _pad_lanes(lam_k2[layer]), jnp.full((1, LANES), lam_init, F32), jnp.zeros((3, LANES), F32)], axis=0)
        sw = row(subln_w[layer])
        wo = w_out[layer].astype(BF16)
        wp = w_ple[layer].astype(BF16)
        wg = w_ple_gate[layer].astype(BF16)
        g, bb, bg = row(ln_g[layer]), row(ln_b[layer]), row(b_ple_gate[layer])

        z, xbc, dt, q, k, v, za, kb, vb = _in_proj(xp, w_r, tm_in)
        r3 = lambda a: a.reshape(b, l, a.shape[-1])
        y_ssd, h_new = _ssd_prompt(r3(xbc), r3(dt), r3(z), cw, cb, dtb, alog, dsk, nw, e_mat, e_mat_t)
        y_att = _attn_prompt(r3(q), r3(kb), r3(vb), r3(za), lam_blk, sw, tq, tk, rs)
        xp = _layer_out(y_ssd.reshape(mp, D_SSD), y_att.reshape(mp, D_ATT), xp,
                        p_prompt[layer].reshape(mp, PLE_DIM), wo, g, bb, wp, wg, bg, tm_out)
        kp.append(k.reshape(b, l, ATT_HEADS, 2, ATT_HEAD_DIM))
        vp.append(v.reshape(b, l, ATT_HEADS, ATT_V_DIM))
        hp.append(h_new.reshape(b, SSD_HEADS, SSD_HEAD_DIM, SSD_STATE))
        cp.append(r3(xbc)[:, l - (SSD_CONV - 1):, :])

        z, xbc, dt, q, k, v, za, _, _ = _in_proj(xs, w_r, db)
        y_ssd, conv_new, h_new = _ssd_sample(xbc, dt, z, state_conv, state_ssm, layer,
                                             cw, cb, dtb, alog, dsk, nw, e_mat)
        y_att = _attn_sample(page_table, q, k, v, za, cache_k, cache_v, layer, lam_blk, sw)
        xs = _layer_out(y_ssd, y_att, xs, p_sample[layer].reshape(db, PLE_DIM), wo, g, bb, wp, wg, bg, db)
        ksl.append(k.reshape(db, 1, ATT_HEADS, 2, ATT_HEAD_DIM))
        vsl.append(v.reshape(db, 1, ATT_HEADS, ATT_V_DIM))
        hsl.append(h_new.reshape(db, SSD_HEADS, SSD_HEAD_DIM, SSD_STATE))
        csl.append(conv_new)

    return (xp.reshape(b, l, D_MODEL), xs.reshape(db, 1, D_MODEL), jnp.stack(kp), jnp.stack(vp),
            jnp.stack(hp), jnp.stack(cp), jnp.stack(ksl), jnp.stack(vsl), jnp.stack(hsl), jnp.stack(csl))
```

```python
import functools
import math

import numpy as np
import jax
import jax.numpy as jnp
from jax import lax
from jax.experimental import pallas as pl
from jax.experimental.pallas import tpu as pltpu

F32 = jnp.float32
BF16 = jnp.bfloat16

D_MODEL = 1024
DEPTH = 4
D_SSD = 1024
SSD_HEAD_DIM = 64
SSD_HEADS = 16
SSD_STATE = 128
SSD_GROUPS = 2
SSD_CONV = 4
SSD_CHUNK = 128
CONV_DIM = D_SSD + 2 * SSD_GROUPS * SSD_STATE
ATT_HEAD_DIM = 64
ATT_V_DIM = 128
ATT_HEADS = 8
D_ATT = 1024
QK_DIM = 1024
ATT_SCALE = ATT_HEAD_DIM ** -0.5
Q_SCALE = ATT_SCALE * math.log2(math.e)
PLE_DIM = 256
DEEPNORM_ALPHA = (2 * DEPTH) ** 0.25
NORM_EPS = 1e-5
LANES = 128
DT_PAD = LANES
C_Z, C_XBC, C_Q, C_K, C_V, C_ZA, C_DT, C_END = 0, 1024, 2560, 3584, 4608, 5632, 6656, 6784
VMEM_LIMIT = 56 * 1024 * 1024


def _cparams(sem):
    return pltpu.CompilerParams(dimension_semantics=sem, vmem_limit_bytes=VMEM_LIMIT)


def _silu(x):
    return x * jax.nn.sigmoid(x)


def _softplus(x):
    return jnp.maximum(x, 0.0) + jnp.log1p(jnp.exp(-jnp.abs(x)))


def _split3(a):
    hi = a.astype(BF16)
    r1 = a - hi.astype(F32)
    mid = r1.astype(BF16)
    lo = (r1 - mid.astype(F32)).astype(BF16)
    return hi, mid, lo


def _dot(a, b):
    return jnp.dot(a, b, preferred_element_type=F32)


def _dot_nt(a, b):
    return lax.dot_general(a, b, (((1,), (1,)), ((), ())), preferred_element_type=F32)


def _dot_sel_rhs(a, sel):
    hi, mid, lo = _split3(a)
    return (_dot(hi, sel) + _dot(mid, sel)) + _dot(lo, sel)


def _dot_sel_lhs(sel, b):
    hi, mid, lo = _split3(b)
    return (_dot(sel, hi) + _dot(sel, mid)) + _dot(sel, lo)


def _in_proj_kernel(x_ref, w_ref, z_ref, xbc_ref, dt_ref, q_ref, k_ref, v_ref, za_ref, kb_ref, vb_ref):
    xb = x_ref[...].astype(BF16)

    def mm(lo, hi):
        return _dot(xb, w_ref[:, lo:hi])

    z_ref[...] = mm(C_Z, C_XBC)
    xbc_ref[...] = mm(C_XBC, C_Q)
    q_ref[...] = (mm(C_Q, C_K) * Q_SCALE).astype(BF16)
    k = mm(C_K, C_V)
    k_ref[...] = k
    kb_ref[...] = k.astype(BF16)
    v = mm(C_V, C_ZA)
    v_ref[...] = v
    vb_ref[...] = v.astype(BF16)
    za_ref[...] = mm(C_ZA, C_DT)
    dt_ref[...] = mm(C_DT, C_END)


def _in_proj(x, w_all, layer, tm):
    m = x.shape[0]
    assert m % tm == 0
    widths = (1024, CONV_DIM, DT_PAD, 1024, 1024, 1024, 1024, 1024, 1024)
    dtypes = (F32, F32, F32, BF16, F32, F32, F32, BF16, BF16)
    out_shape = tuple(jax.ShapeDtypeStruct((m, w), d) for w, d in zip(widths, dtypes))
    out_specs = tuple(pl.BlockSpec((tm, w), lambda i: (i, 0)) for w in widths)
    return pl.pallas_call(
        _in_proj_kernel,
        out_shape=out_shape,
        grid=(m // tm,),
        in_specs=[pl.BlockSpec((tm, D_MODEL), lambda i: (i, 0)),
                  pl.BlockSpec((None, D_MODEL, C_END), lambda i: (layer, 0, 0))],
        out_specs=out_specs,
        compiler_params=_cparams(("arbitrary",)),
        name="in_proj",
    )(x, w_all)


def _in_proj_prompt_kernel(*refs, aliased):
    x_ref, w_ref = refs[0:2]
    z_ref, xbc_ref, dt_ref, q_ref, za_ref, kb_ref, vb_ref, kt_ref, v_ref = refs[4:] if aliased else refs[2:]
    nb, ts, _ = x_ref.shape
    xb = x_ref[...].reshape(nb * ts, D_MODEL).astype(BF16)

    def mm(lo, hi):
        return _dot(xb, w_ref[:, lo:hi])

    def put(ref, val):
        ref[...] = val.reshape(nb, ts, val.shape[-1]).astype(ref.dtype)

    put(z_ref, mm(C_Z, C_XBC))
    put(xbc_ref, mm(C_XBC, C_Q))
    put(q_ref, mm(C_Q, C_K) * Q_SCALE)
    k = mm(C_K, C_V)
    put(kb_ref, k)
    for bi in range(nb):
        kt_ref[bi] = k[bi * ts:(bi + 1) * ts, :].T
    v = mm(C_V, C_ZA)
    put(v_ref, v)
    put(vb_ref, v)
    put(za_ref, mm(C_ZA, C_DT))
    put(dt_ref, mm(C_DT, C_END))


def _in_proj_prompt(x, w_all, layer, kt_all, v_all, ts):
    b, l, _ = x.shape
    assert l % ts == 0
    widths = (1024, CONV_DIM, DT_PAD, 1024, 1024, 1024, 1024)
    dtypes = (F32, F32, F32, BF16, F32, BF16, BF16)
    tok = lambda w: pl.BlockSpec((b, ts, w), lambda i: (0, i, 0))
    out_shape = tuple(jax.ShapeDtypeStruct((b, l, w), d) for w, d in zip(widths, dtypes)) + (
        jax.ShapeDtypeStruct((DEPTH, b, QK_DIM, l), F32), jax.ShapeDtypeStruct((DEPTH, b, l, D_ATT), F32))
    out_specs = tuple(tok(w) for w in widths) + (
        pl.BlockSpec((None, b, QK_DIM, ts), lambda i: (layer, 0, 0, i)),
        pl.BlockSpec((None, b, ts, D_ATT), lambda i: (layer, 0, i, 0)))
    in_specs = [tok(D_MODEL), pl.BlockSpec((None, D_MODEL, C_END), lambda i: (layer, 0, 0))]
    args = [x, w_all]
    aliased = kt_all is not None
    aliases = {}
    if aliased:
        in_specs += [pl.BlockSpec(memory_space=pl.ANY), pl.BlockSpec(memory_space=pl.ANY)]
        args += [kt_all, v_all]
        aliases = {2: len(widths), 3: len(widths) + 1}
    return pl.pallas_call(
        functools.partial(_in_proj_prompt_kernel, aliased=aliased),
        out_shape=out_shape,
        grid=(l // ts,),
        in_specs=in_specs,
        out_specs=out_specs,
        input_output_aliases=aliases,
        compiler_params=_cparams(("arbitrary",)),
        name="in_proj_prompt",
    )(*args)


def _layer_out_kernel(ys_ref, ya_ref, x_ref, p_ref, wo_ref, g_ref, b_ref, wp_ref, wg_ref, bg_ref, o_ref):
    y = _dot(ys_ref[...].astype(BF16), wo_ref[0:D_SSD, :]) + _dot(ya_ref[...].astype(BF16), wo_ref[D_SSD:, :])
    t = DEEPNORM_ALPHA * x_ref[...] + y
    mu = jnp.mean(t, axis=-1, keepdims=True)
    var = jnp.mean(jnp.square(t - mu), axis=-1, keepdims=True)
    h = ((t - mu) * lax.rsqrt(var + NORM_EPS)) * g_ref[...] + b_ref[...]
    gate = jax.nn.sigmoid(_dot(h.astype(BF16), wg_ref[...]) + bg_ref[...])
    o_ref[...] = h + gate * _dot(p_ref[...].astype(BF16), wp_ref[...])


def _layer_out(ys, ya, x, p_all, layer, wo, g, b, wp, wg, bg, tm):
    m = x.shape[0]
    assert m % tm == 0
    row = lambda w: pl.BlockSpec((tm, w), lambda i: (i, 0))
    full = lambda a: pl.BlockSpec(a.shape, lambda i: (0, 0))
    return pl.pallas_call(
        _layer_out_kernel,
        out_shape=jax.ShapeDtypeStruct((m, D_MODEL), F32),
        grid=(m // tm,),
        in_specs=[row(D_SSD), row(D_ATT), row(D_MODEL),
                  pl.BlockSpec((None, tm, PLE_DIM), lambda i: (layer, i, 0)),
                  full(wo), full(g), full(b), full(wp), full(wg), full(bg)],
        out_specs=row(D_MODEL),
        compiler_params=_cparams(("arbitrary",)),
        name="layer_out",
    )(ys, ya, x, p_all, wo, g, b, wp, wg, bg)


def _head_expand_matrix():
    e = np.zeros((LANES, D_SSD), np.float32)
    for h in range(SSD_HEADS):
        e[h, h * SSD_HEAD_DIM:(h + 1) * SSD_HEAD_DIM] = 1.0
    return e


def _ssd_prompt_kernel(xbc_ref, dt_ref, z_ref, cw_ref, cb_ref, dtb_ref, alog_ref, dsk_ref, nw_ref,
                       exp_ref, expt_ref, y_ref, h_ref, xpad_ref):
    cl = SSD_CHUNK
    c = pl.program_id(1)

    @pl.when(c == 0)
    def _():
        xpad_ref[0:8, :] = jnp.zeros((8, CONV_DIM), F32)
        h_ref[...] = jnp.zeros(h_ref.shape, F32)

    xpad_ref[8:8 + cl, :] = xbc_ref[0]
    acc = cb_ref[...] + cw_ref[0:1, :] * xpad_ref[5:5 + cl, :]
    acc = acc + cw_ref[1:2, :] * xpad_ref[6:6 + cl, :]
    acc = acc + cw_ref[2:3, :] * xpad_ref[7:7 + cl, :]
    acc = acc + cw_ref[3:4, :] * xpad_ref[8:8 + cl, :]
    xpad_ref[5:8, :] = xpad_ref[5 + cl:8 + cl, :]
    act = _silu(acc)
    xs = act[:, 0:D_SSD]
    gn = SSD_STATE
    b_g = [act[:, D_SSD + g * gn:D_SSD + (g + 1) * gn] for g in range(SSD_GROUPS)]
    c_g = [act[:, D_SSD + (SSD_GROUPS + g) * gn:D_SSD + (SSD_GROUPS + g + 1) * gn] for g in range(SSD_GROUPS)]

    dt = _softplus(dt_ref[0] + dtb_ref[...])
    a_neg = -jnp.exp(alog_ref[...])
    d_a = dt * a_neg
    row = lax.broadcasted_iota(jnp.int32, (cl, cl), 0)
    col = lax.broadcasted_iota(jnp.int32, (cl, cl), 1)
    causal = col <= row
    tril = jnp.where(causal, 1.0, 0.0).astype(BF16)
    acum = _dot_sel_lhs(tril, d_a)
    acum_t = acum.T
    dt_t = dt.T
    a_last = acum[cl - 1:cl, :]

    e_mat = exp_ref[...]
    e_acum = _dot_sel_rhs(jnp.exp(acum), e_mat)
    e_rest = _dot_sel_rhs(jnp.exp(a_last - acum) * dt, e_mat)
    xw = xs * e_rest
    cd_col = jnp.broadcast_to(jnp.exp(acum_t[:, cl - 1:cl]), (LANES, LANES))
    cd_full = _dot_sel_lhs(expt_ref[...], cd_col)

    lane = lax.broadcasted_iota(jnp.int32, (cl, LANES), 1)
    lo_half = lane < SSD_HEAD_DIM
    heads_per_group = SSD_HEADS // SSD_GROUPS
    gw = heads_per_group * SSD_HEAD_DIM
    y_parts = []
    for g in range(SSD_GROUPS):
        cbf = c_g[g].astype(BF16)
        bbf = b_g[g].astype(BF16)
        cb = _dot_nt(cbf, bbf)
        h_old = h_ref[0, g * gw:(g + 1) * gw, :]
        y_off = _dot_nt(cbf, h_old.astype(BF16))
        for hp in range(heads_per_group // 2):
            x_pair = xs[:, g * gw + hp * LANES:g * gw + (hp + 1) * LANES]
            pair = None
            for s in range(2):
                h = g * heads_per_group + 2 * hp + s
                seg = acum[:, h:h + 1] - acum_t[h:h + 1, :]
                decay = jnp.exp(jnp.where(causal, seg, -jnp.inf))
                w = (cb * decay) * dt_t[h:h + 1, :]
                x_half = jnp.where(lo_half if s == 0 else jnp.logical_not(lo_half), x_pair, 0.0)
                yd = _dot(w.astype(BF16), x_half.astype(BF16))
                pair = yd if pair is None else pair + yd
            y_parts.append(pair + y_off[:, hp * LANES:(hp + 1) * LANES]
                           * e_acum[:, g * gw + hp * LANES:g * gw + (hp + 1) * LANES])
        st = _dot(xw[:, g * gw:(g + 1) * gw].T.astype(BF16), bbf)
        h_ref[0, g * gw:(g + 1) * gw, :] = cd_full[g * gw:(g + 1) * gw, :] * h_old + st

    y = jnp.concatenate(y_parts, axis=1) + dsk_ref[...] * xs
    y = y * _silu(z_ref[0])
    outs = []
    for g in range(SSD_GROUPS):
        yg = y[:, g * gw:(g + 1) * gw]
        ms = jnp.mean(jnp.square(yg), axis=-1, keepdims=True)
        outs.append(yg * lax.rsqrt(ms + NORM_EPS) * nw_ref[:, g * gw:(g + 1) * gw])
    y_ref[0] = jnp.concatenate(outs, axis=1).astype(y_ref.dtype)


def _ssd_prompt(xbc, dt, z, cw, cb, dtb, alog, dsk, nw, e_mat, e_mat_t):
    b, l, _ = xbc.shape
    assert l % SSD_CHUNK == 0
    nc = l // SSD_CHUNK
    tok = lambda w: pl.BlockSpec((1, SSD_CHUNK, w), lambda i, j: (i, j, 0))
    full = lambda a: pl.BlockSpec(a.shape, lambda i, j: (0,) * a.ndim)
    return pl.pallas_call(
        _ssd_prompt_kernel,
        out_shape=(jax.ShapeDtypeStruct((b, l, D_SSD), BF16),
                   jax.ShapeDtypeStruct((b, SSD_HEADS * SSD_HEAD_DIM, SSD_STATE), F32)),
        grid=(b, nc),
        in_specs=[tok(CONV_DIM), tok(DT_PAD), tok(D_SSD), full(cw), full(cb), full(dtb), full(alog),
                  full(dsk), full(nw), full(e_mat), full(e_mat_t)],
        out_specs=(tok(D_SSD),
                   pl.BlockSpec((1, SSD_HEADS * SSD_HEAD_DIM, SSD_STATE), lambda i, j: (i, 0, 0))),
        scratch_shapes=[pltpu.VMEM((8 + SSD_CHUNK, CONV_DIM), F32)],
        compiler_params=_cparams(("arbitrary", "arbitrary")),
        name="ssd_prompt",
    )(xbc, dt, z, cw, cb, dtb, alog, dsk, nw, e_mat, e_mat_t)


def _lambda_from(lam_ref):
    s1 = jnp.sum(lam_ref[0:1, :] * lam_ref[1:2, :], axis=-1, keepdims=True)
    s2 = jnp.sum(lam_ref[2:3, :] * lam_ref[3:4, :], axis=-1, keepdims=True)
    lam_init = lam_ref[4:5, 0:1]
    return jnp.exp(s1) - jnp.exp(s2) + lam_init, lam_init


def _attn_prompt_kernel(q_ref, k_ref, v_ref, za_ref, lam_ref, sw_ref, o_ref, qst_ref, vt_ref, m_ref, l_ref, acc_ref,
                        sa_ref, sb_ref, *, tq, tk, rs):
    qi = pl.program_id(2)
    nkb = vt_ref.shape[0]

    @pl.when(qi == 0)
    def _():
        for c in range(nkb):
            vt_ref[c] = v_ref[0, c * tk:(c + 1) * tk, :].astype(F32).T.astype(BF16)

    qt = q_ref[0].astype(F32).T
    rowd = lax.broadcasted_iota(jnp.int32, (LANES, tq), 0)
    qst_ref[:, 0:tq] = jnp.where(rowd < ATT_HEAD_DIM, qt, 0.0).astype(BF16)
    qst_ref[:, tq:2 * tq] = jnp.where(rowd >= ATT_HEAD_DIM, qt, 0.0).astype(BF16)
    m_ref[...] = jnp.full(m_ref.shape, -jnp.inf, F32)
    l_ref[...] = jnp.zeros(l_ref.shape, F32)
    acc_ref[...] = jnp.zeros(acc_ref.shape, F32)
    n_strips = (2 * tq) // rs

    def strips_for(diag_off):
        out = []
        for r in range(n_strips):
            q_lo = (r * rs) % tq
            if diag_off is not None and diag_off > q_lo + rs - 1:
                continue
            out.append((slice(r * rs, (r + 1) * rs), q_lo))
        return out

    def scores_into(buf, ki, diag_off):
        start = pl.multiple_of(ki * tk, tk)
        k = k_ref[0, pl.ds(start, tk), :]
        for sl, _ in strips_for(diag_off):
            buf[:, sl] = _dot(k, qst_ref[:, sl])

    def consume(buf, ki, diag_off):
        vt = vt_ref[ki]
        strips = strips_for(diag_off)
        probs = []
        for sl, q_lo in strips:
            s = buf[:, sl]
            if diag_off is not None:
                kk = lax.broadcasted_iota(jnp.int32, (tk, rs), 0) + diag_off
                qq = lax.broadcasted_iota(jnp.int32, (tk, rs), 1) + q_lo
                s = jnp.where(kk <= qq, s, -jnp.inf)
            m_prev = m_ref[:, sl]
            m_new = jnp.maximum(m_prev, jnp.max(s, axis=0, keepdims=True))
            alpha = jnp.exp2(m_prev - m_new)
            p = jnp.exp2(s - m_new)
            l_ref[:, sl] = alpha * l_ref[:, sl] + jnp.sum(p, axis=0, keepdims=True)
            m_ref[:, sl] = m_new
            probs.append((alpha, p.astype(BF16)))
        for (sl, _), (alpha, pb) in zip(strips, probs):
            acc_ref[:, sl] = alpha * acc_ref[:, sl] + _dot(vt, pb)

    def phase(cur, nxt, ki, diag_cur, diag_next, has_next=True):
        if has_next:
            scores_into(nxt, ki + 1, diag_next)
        consume(cur, ki, diag_cur)

    scores_into(sa_ref, 0, None)

    def body(j, carry):
        phase(sa_ref, sb_ref, 2 * j, None, None)
        phase(sb_ref, sa_ref, 2 * j + 1, None, None)
        return carry

    lax.fori_loop(0, qi, body, 0)
    phase(sa_ref, sb_ref, 2 * qi, 0, tk)
    phase(sb_ref, sa_ref, 2 * qi + 1, tk, None, has_next=False)

    ot = acc_ref[...] * (1.0 / l_ref[...])
    lam, lam_init = _lambda_from(lam_ref)
    od = (ot[:, 0:tq] - lam * ot[:, tq:2 * tq]).T
    ms = jnp.mean(jnp.square(od), axis=-1, keepdims=True)
    on = od * lax.rsqrt(ms + NORM_EPS) * sw_ref[...] * (1.0 - lam_init)
    o_ref[0] = (on * _silu(za_ref[0])).astype(o_ref.dtype)


def _attn_prompt(q, kb, vb, za, lam_blk, sw, tq, tk, rs):
    b, l, _ = q.shape
    assert l % tq == 0 and tq == 2 * tk and tq % rs == 0
    nq = l // tq
    qspec = pl.BlockSpec((1, tq, LANES), lambda bi, h, i: (bi, i, h))
    kspec = pl.BlockSpec((1, l, LANES), lambda bi, h, i: (bi, 0, h))
    full = lambda a: pl.BlockSpec(a.shape, lambda bi, h, i: (0, 0))
    return pl.pallas_call(
        functools.partial(_attn_prompt_kernel, tq=tq, tk=tk, rs=rs),
        out_shape=jax.ShapeDtypeStruct((b, l, D_ATT), BF16),
        grid=(b, ATT_HEADS, nq),
        in_specs=[qspec, kspec, kspec, qspec, full(lam_blk), full(sw)],
        out_specs=qspec,
        scratch_shapes=[pltpu.VMEM((LANES, 2 * tq), BF16), pltpu.VMEM((l // tk, ATT_V_DIM, tk), BF16),
                        pltpu.VMEM((1, 2 * tq), F32), pltpu.VMEM((1, 2 * tq), F32),
                        pltpu.VMEM((ATT_V_DIM, 2 * tq), F32),
                        pltpu.VMEM((tk, 2 * tq), F32), pltpu.VMEM((tk, 2 * tq), F32)],
        compiler_params=_cparams(("arbitrary", "arbitrary", "arbitrary")),
        name="attn_prompt",
    )(q, kb, vb, za, lam_blk, sw)


def _ssd_sample_kernel(xbc_ref, dt_ref, z_ref, sc_ref, h0_ref, cw_ref, cb_ref, dtb_ref, alog_ref, dsk_ref,
                       nw_ref, exp_ref, y_ref, cn_ref, h_ref, *, nb):
    x_new = xbc_ref[...]
    c0, c1, c2 = sc_ref[:, 0, :], sc_ref[:, 1, :], sc_ref[:, 2, :]
    acc = cb_ref[...] + cw_ref[0:1, :] * c0
    acc = acc + cw_ref[1:2, :] * c1
    acc = acc + cw_ref[2:3, :] * c2
    acc = acc + cw_ref[3:4, :] * x_new
    cn_ref[:, 0, :] = c1
    cn_ref[:, 1, :] = c2
    cn_ref[:, 2, :] = x_new
    act = _silu(acc)
    xs = act[:, 0:D_SSD]
    gn = SSD_STATE
    b_g = [act[:, D_SSD + g * gn:D_SSD + (g + 1) * gn] for g in range(SSD_GROUPS)]
    c_g = [act[:, D_SSD + (SSD_GROUPS + g) * gn:D_SSD + (SSD_GROUPS + g + 1) * gn] for g in range(SSD_GROUPS)]

    dt = _softplus(dt_ref[...] + dtb_ref[...])
    d_a = dt * (-jnp.exp(alog_ref[...]))
    e_mat = exp_ref[...]
    dt_e = _dot_sel_rhs(dt, e_mat)
    dec_e = _dot_sel_rhs(jnp.exp(d_a), e_mat)
    xdt = xs * dt_e

    heads_per_group = SSD_HEADS // SSD_GROUPS
    gw = heads_per_group * SSD_HEAD_DIM
    pad = jnp.zeros((LANES - nb, D_SSD), F32)
    xdt_t = jnp.concatenate([xdt, pad], axis=0).T
    dec_t = jnp.concatenate([dec_e, pad], axis=0).T
    rowid = lax.broadcasted_iota(jnp.int32, (nb, gw), 0)
    y_parts = []
    for g in range(SSD_GROUPS):
        cbf = c_g[g].astype(BF16)
        cbdot = jnp.sum(c_g[g] * b_g[g], axis=-1, keepdims=True)
        y_off = jnp.zeros((nb, gw), F32)
        for j in range(nb):
            h_old = h0_ref[j, g * gw:(g + 1) * gw, :]
            yo = _dot_nt(cbf, h_old.astype(BF16))
            y_off = jnp.where(rowid == j, yo, y_off)
            xcol = xdt_t[g * gw:(g + 1) * gw, j:j + 1]
            dcol = dec_t[g * gw:(g + 1) * gw, j:j + 1]
            h_ref[j, g * gw:(g + 1) * gw, :] = dcol * h_old + xcol * b_g[g][j:j + 1, :]
        sl = slice(g * gw, (g + 1) * gw)
        y_parts.append((cbdot * dt_e[:, sl]) * xs[:, sl] + y_off * dec_e[:, sl])
    y = jnp.concatenate(y_parts, axis=1) + dsk_ref[...] * xs
    y = y * _silu(z_ref[...])
    outs = []
    for g in range(SSD_GROUPS):
        yg = y[:, g * gw:(g + 1) * gw]
        ms = jnp.mean(jnp.square(yg), axis=-1, keepdims=True)
        outs.append(yg * lax.rsqrt(ms + NORM_EPS) * nw_ref[:, g * gw:(g + 1) * gw])
    y_ref[...] = jnp.concatenate(outs, axis=1).astype(y_ref.dtype)


def _ssd_sample(xbc, dt, z, state_conv, state_ssm, layer, cw, cb, dtb, alog, dsk, nw, e_mat, nb=8):
    db = xbc.shape[0]
    assert db % nb == 0
    hp = SSD_HEADS * SSD_HEAD_DIM
    ssm = state_ssm.reshape(DEPTH, db, hp, SSD_STATE)
    tok = lambda w: pl.BlockSpec((nb, w), lambda i: (i, 0))
    full = lambda a: pl.BlockSpec(a.shape, lambda i: (0,) * a.ndim)
    return pl.pallas_call(
        functools.partial(_ssd_sample_kernel, nb=nb),
        out_shape=(jax.ShapeDtypeStruct((db, D_SSD), BF16),
                   jax.ShapeDtypeStruct((db, SSD_CONV - 1, CONV_DIM), F32),
                   jax.ShapeDtypeStruct((db, hp, SSD_STATE), F32)),
        grid=(db // nb,),
        in_specs=[tok(CONV_DIM), tok(DT_PAD), tok(D_SSD),
                  pl.BlockSpec((None, nb, SSD_CONV - 1, CONV_DIM), lambda i: (layer, i, 0, 0)),
                  pl.BlockSpec((None, nb, hp, SSD_STATE), lambda i: (layer, i, 0, 0)),
                  full(cw), full(cb), full(dtb), full(alog), full(dsk), full(nw), full(e_mat)],
        out_specs=(tok(D_SSD),
                   pl.BlockSpec((nb, SSD_CONV - 1, CONV_DIM), lambda i: (i, 0, 0)),
                   pl.BlockSpec((nb, hp, SSD_STATE), lambda i: (i, 0, 0))),
        compiler_params=_cparams(("arbitrary",)),
        name="ssd_sample",
    )(xbc, dt, z, state_conv, ssm, cw, cb, dtb, alog, dsk, nw, e_mat)


N_QROWS = 2 * ATT_HEADS


def _attn_sample_kernel(pt_ref, q_ref, kn_ref, vn_ref, za_ref, lam_ref, sw_ref, rexp_ref, *rest, pp):
    k_refs = rest[0:pp]
    v_refs = rest[pp:2 * pp]
    o_ref = rest[2 * pp]
    qsel_ref, m_ref, l_ref, acc_ref = rest[2 * pp + 1:]
    j = pl.program_id(1)
    nj = pl.num_programs(1)

    @pl.when(j == 0)
    def _():
        rowi = lax.broadcasted_iota(jnp.int32, (N_QROWS, QK_DIM), 0)
        coli = lax.broadcasted_iota(jnp.int32, (N_QROWS, QK_DIM), 1)
        own = (coli >> 6) == (2 * (rowi & (ATT_HEADS - 1)) + (rowi >> 3))
        qb = jnp.broadcast_to(q_ref[0].astype(F32), (N_QROWS, QK_DIM))
        qsel_ref[...] = jnp.where(own, qb, 0.0).astype(BF16)
        m_ref[...] = jnp.full(m_ref.shape, -jnp.inf, F32)
        l_ref[...] = jnp.zeros(l_ref.shape, F32)
        acc_ref[...] = jnp.zeros(acc_ref.shape, F32)

    qsel = qsel_ref[...]
    page = k_refs[0].shape[1]
    s = jnp.concatenate([_dot(qsel, k_refs[i][...].astype(BF16)) for i in range(pp)], axis=1)
    m_prev = m_ref[...]
    m_new = jnp.maximum(m_prev, jnp.max(s, axis=-1, keepdims=True))
    alpha = jnp.exp2(m_prev - m_new)
    p = jnp.exp2(s - m_new)
    l_ref[...] = alpha * l_ref[...] + jnp.sum(p, axis=-1, keepdims=True)
    pb = p.astype(BF16)
    p_stack = jnp.concatenate([pb[:, i * page:(i + 1) * page] for i in range(pp)], axis=0)
    p_exp = _dot(p_stack, rexp_ref[...])
    rowe = lax.broadcasted_iota(jnp.int32, p_exp.shape, 0)
    cole = lax.broadcasted_iota(jnp.int32, p_exp.shape, 1)
    keep = (cole & (ATT_HEADS - 1)) == (rowe & (ATT_HEADS - 1))
    p2 = jnp.where(keep, p_exp, 0.0).astype(BF16)
    pv = _dot(p2[0:N_QROWS, :], v_refs[0][...].astype(BF16))
    for i in range(1, pp):
        pv = pv + _dot(p2[i * N_QROWS:(i + 1) * N_QROWS, :], v_refs[i][...].astype(BF16))
    acc_ref[...] = alpha * acc_ref[...] + pv
    m_ref[...] = m_new

    @pl.when(j == nj - 1)
    def _():
        kn = jnp.broadcast_to(kn_ref[0], (N_QROWS, QK_DIM))
        s_new = jnp.sum(qsel_ref[...].astype(F32) * kn, axis=-1, keepdims=True)
        m_p = m_ref[...]
        m_f = jnp.maximum(m_p, s_new)
        a_f = jnp.exp2(m_p - m_f)
        p_new = jnp.exp2(s_new - m_f)
        l_f = a_f * l_ref[...] + p_new
        vn = jnp.concatenate([vn_ref[0], vn_ref[0]], axis=0)
        o = (a_f * acc_ref[...] + p_new * vn) / l_f
        lam, lam_init = _lambda_from(lam_ref)
        od = o[0:ATT_HEADS, :] - lam * o[ATT_HEADS:N_QROWS, :]
        ms = jnp.mean(jnp.square(od), axis=-1, keepdims=True)
        on = od * lax.rsqrt(ms + NORM_EPS) * sw_ref[...] * (1.0 - lam_init)
        o_ref[0] = on * _silu(za_ref[0])


def _attn_sample(page_table, q, k_new, v_new, za, cache_k, cache_v, layer, lam_blk, sw, pp=8):
    db, n_pages = page_table.shape
    assert n_pages % pp == 0
    n_pool, page = cache_k.shape[1], cache_k.shape[2]
    r_np = np.zeros((page, page * ATT_HEADS), np.float32)
    for key in range(page):
        r_np[key, key * ATT_HEADS:(key + 1) * ATT_HEADS] = 1.0
    rexp = jnp.asarray(r_np, BF16)
    ck = jnp.transpose(cache_k, (0, 1, 3, 4, 5, 2)).reshape(DEPTH, n_pool, QK_DIM, page)
    cv = cache_v.reshape(DEPTH, n_pool, page * ATT_HEADS, ATT_V_DIM)
    tok = lambda w: pl.BlockSpec((1, 1, w), lambda b, j, pt: (b, 0, 0))
    tokh = pl.BlockSpec((1, ATT_HEADS, ATT_V_DIM), lambda b, j, pt: (b, 0, 0))
    full = lambda a: pl.BlockSpec(a.shape, lambda b, j, pt: (0, 0))

    def page_spec(i, rows, w):
        return pl.BlockSpec((None, None, rows, w), lambda b, j, pt: (layer, pt[b, j * pp + i], 0, 0))

    grid_spec = pltpu.PrefetchScalarGridSpec(
        num_scalar_prefetch=1,
        grid=(db, n_pages // pp),
        in_specs=[tok(QK_DIM), tok(QK_DIM), tokh, tokh, full(lam_blk), full(sw), full(rexp)]
        + [page_spec(i, QK_DIM, page) for i in range(pp)]
        + [page_spec(i, page * ATT_HEADS, ATT_V_DIM) for i in range(pp)],
        out_specs=tokh,
        scratch_shapes=[pltpu.VMEM((N_QROWS, QK_DIM), BF16), pltpu.VMEM((N_QROWS, 1), F32),
                        pltpu.VMEM((N_QROWS, 1), F32), pltpu.VMEM((N_QROWS, ATT_V_DIM), F32)],
    )
    r3 = lambda a: a.reshape(db, 1, a.shape[-1])
    rh = lambda a: a.reshape(db, ATT_HEADS, ATT_V_DIM)
    out = pl.pallas_call(
        functools.partial(_attn_sample_kernel, pp=pp),
        out_shape=jax.ShapeDtypeStruct((db, ATT_HEADS, ATT_V_DIM), F32),
        grid_spec=grid_spec,
        compiler_params=_cparams(("arbitrary", "arbitrary")),
        name="attn_sample",
    )(page_table, r3(q), r3(k_new), rh(v_new), rh(za), lam_blk, sw, rexp, *([ck] * pp), *([cv] * pp))
    return out.reshape(db, D_ATT)


def _pad_lanes(v, width=LANES):
    v = v.reshape(1, -1)
    return jnp.pad(v, ((0, 0), (0, width - v.shape[1])))


def _pick_tile(m, pref):
    t = min(pref, m)
    while m % t:
        t //= 2
    return t


def kernel(x_prompt, x_sample, cache_k, cache_v, state_ssm, state_conv, page_table, p_prompt, p_sample,
           w_in, conv_w, conv_b, dt_bias, a_log, d_skip, ssd_norm_w, lam_q1, lam_k1, lam_q2, lam_k2,
           subln_w, w_out, ln_g, ln_b, w_ple, w_ple_gate, b_ple_gate):
    b, l, _ = x_prompt.shape
    db = x_sample.shape[0]
    mp = b * l
    xp = x_prompt.reshape(mp, D_MODEL)
    xs = x_sample.reshape(db, D_MODEL)
    e_np = _head_expand_matrix()
    e_mat = jnp.asarray(e_np, BF16)
    e_mat_t = jnp.asarray(e_np.T, BF16)
    ts_in = _pick_tile(l, 256 // b)
    tm_out = _pick_tile(mp, 512)
    tq = _pick_tile(l, 1024)
    tk = tq // 2
    rs = _pick_tile(tk, 256)
    row = lambda v: v.reshape(1, -1)
    dt_lo = D_SSD + CONV_DIM
    w_all = jnp.concatenate(
        [w_in[:, :, :dt_lo], w_in[:, :, dt_lo + SSD_HEADS:], w_in[:, :, dt_lo:dt_lo + SSD_HEADS],
         jnp.zeros((DEPTH, D_MODEL, DT_PAD - SSD_HEADS), w_in.dtype)], axis=2).astype(BF16)
    pp_all = p_prompt.reshape(DEPTH, mp, PLE_DIM)
    ps_all = p_sample.reshape(DEPTH, db, PLE_DIM)

    kt_all, v_all = None, None
    hp, cp, ksl, vsl, hsl, csl = [], [], [], [], [], []
    for layer in range(DEPTH):
        lam_init = 0.8 - 0.6 * math.exp(-0.3 * layer)
        cw, cb = conv_w[layer], row(conv_b[layer])
        dtb, alog = _pad_lanes(dt_bias[layer]), _pad_lanes(a_log[layer])
        dsk = row(jnp.repeat(d_skip[layer], SSD_HEAD_DIM))
        nw = row(ssd_norm_w[layer])
        lam_blk = jnp.concatenate(
            [_pad_lanes(lam_q1[layer]), _pad_lanes(lam_k1[layer]), _pad_lanes(lam_q2[layer]),
             _pad_lanes(lam_k2[layer]), jnp.full((1, LANES), lam_init, F32), jnp.zeros((3, LANES), F32)], axis=0)
        sw = row(subln_w[layer])
        wo = w_out[layer].astype(BF16)
        wp = w_ple[layer].astype(BF16)
        wg = w_ple_gate[layer].astype(BF16)
        g, bb, bg = row(ln_g[layer]), row(ln_b[layer]), row(b_ple_gate[layer])

        z, xbc, dt, q, za, kb, vb, kt_all, v_all = _in_proj_prompt(
            xp.reshape(b, l, D_MODEL), w_all, layer, kt_all, v_all, ts_in)
        y_ssd, h_new = _ssd_prompt(xbc, dt, z, cw, cb, dtb, alog, dsk, nw, e_mat, e_mat_t)
        y_att = _attn_prompt(q, kb, vb, za, lam_blk, sw, tq, tk, rs)
        xp = _layer_out(y_ssd.reshape(mp, D_SSD), y_att.reshape(mp, D_ATT), xp,
                        pp_all, layer, wo, g, bb, wp, wg, bg, tm_out)
        hp.append(h_new.reshape(b, SSD_HEADS, SSD_HEAD_DIM, SSD_STATE))
        cp.append(xbc[:, l - (SSD_CONV - 1):, :])

        z, xbc, dt, q, k, v, za, _, _ = _in_proj(xs, w_all, layer, db)
        y_ssd, conv_new, h_new = _ssd_sample(xbc, dt, z, state_conv, state_ssm, layer,
                                             cw, cb, dtb, alog, dsk, nw, e_mat)
        y_att = _attn_sample(page_table, q, k, v, za, cache_k, cache_v, layer, lam_blk, sw)
        xs = _layer_out(y_ssd, y_att, xs, ps_all, layer, wo, g, bb, wp, wg, bg, db)
        ksl.append(k.reshape(db, 1, ATT_HEADS, 2, ATT_HEAD_DIM))
        vsl.append(v.reshape(db, 1, ATT_HEADS, ATT_V_DIM))
        hsl.append(h_new.reshape(db, SSD_HEADS, SSD_HEAD_DIM, SSD_STATE))
        csl.append(conv_new)

    k_prompt = kt_all.reshape(DEPTH, b, ATT_HEADS, 2, ATT_HEAD_DIM, l).transpose(0, 1, 5, 2, 3, 4)
    v_prompt = v_all.reshape(DEPTH, b, l, ATT_HEADS, ATT_V_DIM)
    return (xp.reshape(b, l, D_MODEL), xs.reshape(db, 1, D_MODEL), k_prompt, v_prompt,
            jnp.stack(hp), jnp.stack(cp), jnp.stack(ksl), jnp.stack(vsl), jnp.stack(hsl), jnp.stack(csl))
```

```python
import functools
import math

import numpy as np
import jax
import jax.numpy as jnp
from jax import lax
from jax.experimental import pallas as pl
from jax.experimental.pallas import tpu as pltpu

F32 = jnp.float32
BF16 = jnp.bfloat16

D_MODEL = 1024
DEPTH = 4
D_SSD = 1024
SSD_HEAD_DIM = 64
SSD_HEADS = 16
SSD_STATE = 128
SSD_GROUPS = 2
SSD_CONV = 4
SSD_CHUNK = 128
CONV_DIM = D_SSD + 2 * SSD_GROUPS * SSD_STATE
ATT_HEAD_DIM = 64
ATT_V_DIM = 128
ATT_HEADS = 8
D_ATT = 1024
QK_DIM = 1024
ATT_SCALE = ATT_HEAD_DIM ** -0.5
Q_SCALE = ATT_SCALE * math.log2(math.e)
PLE_DIM = 256
DEEPNORM_ALPHA = (2 * DEPTH) ** 0.25
NORM_EPS = 1e-5
LANES = 128
DT_PAD = LANES
C_Z, C_XBC, C_Q, C_K, C_V, C_ZA, C_DT, C_END = 0, 1024, 2560, 3584, 4608, 5632, 6656, 6784
VMEM_LIMIT = 56 * 1024 * 1024


def _cparams(sem):
    return pltpu.CompilerParams(dimension_semantics=sem, vmem_limit_bytes=VMEM_LIMIT)


def _silu(x):
    return x * jax.nn.sigmoid(x)


def _softplus(x):
    return jnp.maximum(x, 0.0) + jnp.log1p(jnp.exp(-jnp.abs(x)))


def _split3(a):
    hi = a.astype(BF16)
    r1 = a - hi.astype(F32)
    mid = r1.astype(BF16)
    lo = (r1 - mid.astype(F32)).astype(BF16)
    return hi, mid, lo


def _dot(a, b):
    return jnp.dot(a, b, preferred_element_type=F32)


def _dot_nt(a, b):
    return lax.dot_general(a, b, (((1,), (1,)), ((), ())), preferred_element_type=F32)


def _dot_sel_rhs(a, sel):
    hi, mid, lo = _split3(a)
    return (_dot(hi, sel) + _dot(mid, sel)) + _dot(lo, sel)


def _dot_sel_lhs(sel, b):
    hi, mid, lo = _split3(b)
    return (_dot(sel, hi) + _dot(sel, mid)) + _dot(sel, lo)


def _in_proj_kernel(x_ref, w_ref, z_ref, xbc_ref, dt_ref, q_ref, k_ref, v_ref, za_ref, kb_ref, vb_ref):
    xb = x_ref[...].astype(BF16)

    def mm(lo, hi):
        return _dot(xb, w_ref[:, lo:hi])

    z_ref[...] = mm(C_Z, C_XBC)
    xbc_ref[...] = mm(C_XBC, C_Q)
    q_ref[...] = (mm(C_Q, C_K) * Q_SCALE).astype(BF16)
    k = mm(C_K, C_V)
    k_ref[...] = k
    kb_ref[...] = k.astype(BF16)
    v = mm(C_V, C_ZA)
    v_ref[...] = v
    vb_ref[...] = v.astype(BF16)
    za_ref[...] = mm(C_ZA, C_DT)
    dt_ref[...] = mm(C_DT, C_END)


def _in_proj(x, w_all, layer, tm):
    m = x.shape[0]
    assert m % tm == 0
    widths = (1024, CONV_DIM, DT_PAD, 1024, 1024, 1024, 1024, 1024, 1024)
    dtypes = (F32, F32, F32, BF16, F32, F32, F32, BF16, BF16)
    out_shape = tuple(jax.ShapeDtypeStruct((m, w), d) for w, d in zip(widths, dtypes))
    out_specs = tuple(pl.BlockSpec((tm, w), lambda i: (i, 0)) for w in widths)
    return pl.pallas_call(
        _in_proj_kernel,
        out_shape=out_shape,
        grid=(m // tm,),
        in_specs=[pl.BlockSpec((tm, D_MODEL), lambda i: (i, 0)),
                  pl.BlockSpec((None, D_MODEL, C_END), lambda i: (layer, 0, 0))],
        out_specs=out_specs,
        compiler_params=_cparams(("arbitrary",)),
        name="in_proj",
    )(x, w_all)


def _in_proj_prompt_kernel(*refs, aliased):
    x_ref, w_ref = refs[0:2]
    z_ref, xbc_ref, dt_ref, q_ref, za_ref, kb_ref, vb_ref, kt_ref, v_ref = refs[4:] if aliased else refs[2:]
    nb, ts, _ = x_ref.shape
    xb = x_ref[...].reshape(nb * ts, D_MODEL).astype(BF16)

    def mm(lo, hi):
        return _dot(xb, w_ref[:, lo:hi])

    def put(ref, val):
        ref[...] = val.reshape(nb, ts, val.shape[-1]).astype(ref.dtype)

    put(z_ref, mm(C_Z, C_XBC))
    put(xbc_ref, mm(C_XBC, C_Q))
    put(q_ref, mm(C_Q, C_K) * Q_SCALE)
    k = mm(C_K, C_V)
    put(kb_ref, k)
    for bi in range(nb):
        kt_ref[bi] = k[bi * ts:(bi + 1) * ts, :].T
    v = mm(C_V, C_ZA)
    put(v_ref, v)
    put(vb_ref, v)
    put(za_ref, mm(C_ZA, C_DT))
    put(dt_ref, mm(C_DT, C_END))


def _in_proj_prompt(x, w_all, layer, kt_all, v_all, ts):
    b, l, _ = x.shape
    assert l % ts == 0
    widths = (1024, CONV_DIM, DT_PAD, 1024, 1024, 1024, 1024)
    dtypes = (F32, F32, F32, BF16, F32, BF16, BF16)
    tok = lambda w: pl.BlockSpec((b, ts, w), lambda i: (0, i, 0))
    out_shape = tuple(jax.ShapeDtypeStruct((b, l, w), d) for w, d in zip(widths, dtypes)) + (
        jax.ShapeDtypeStruct((DEPTH, b, QK_DIM, l), F32), jax.ShapeDtypeStruct((DEPTH, b, l, D_ATT), F32))
    out_specs = tuple(tok(w) for w in widths) + (
        pl.BlockSpec((None, b, QK_DIM, ts), lambda i: (layer, 0, 0, i)),
        pl.BlockSpec((None, b, ts, D_ATT), lambda i: (layer, 0, i, 0)))
    in_specs = [tok(D_MODEL), pl.BlockSpec((None, D_MODEL, C_END), lambda i: (layer, 0, 0))]
    args = [x, w_all]
    aliased = kt_all is not None
    aliases = {}
    if aliased:
        in_specs += [pl.BlockSpec(memory_space=pl.ANY), pl.BlockSpec(memory_space=pl.ANY)]
        args += [kt_all, v_all]
        aliases = {2: len(widths), 3: len(widths) + 1}
    return pl.pallas_call(
        functools.partial(_in_proj_prompt_kernel, aliased=aliased),
        out_shape=out_shape,
        grid=(l // ts,),
        in_specs=in_specs,
        out_specs=out_specs,
        input_output_aliases=aliases,
        compiler_params=_cparams(("arbitrary",)),
        name="in_proj_prompt",
    )(*args)


def _layer_out_kernel(ys_ref, ya_ref, x_ref, p_ref, wo_ref, g_ref, b_ref, wp_ref, wg_ref, bg_ref, o_ref):
    y = _dot(ys_ref[...].astype(BF16), wo_ref[0:D_SSD, :]) + _dot(ya_ref[...].astype(BF16), wo_ref[D_SSD:, :])
    t = DEEPNORM_ALPHA * x_ref[...] + y
    mu = jnp.mean(t, axis=-1, keepdims=True)
    var = jnp.mean(jnp.square(t - mu), axis=-1, keepdims=True)
    h = ((t - mu) * lax.rsqrt(var + NORM_EPS)) * g_ref[...] + b_ref[...]
    gate = jax.nn.sigmoid(_dot(h.astype(BF16), wg_ref[...]) + bg_ref[...])
    o_ref[...] = h + gate * _dot(p_ref[...].astype(BF16), wp_ref[...])


def _layer_out(ys, ya, x, p_all, layer, wo, g, b, wp, wg, bg, tm):
    m = x.shape[0]
    assert m % tm == 0
    row = lambda w: pl.BlockSpec((tm, w), lambda i: (i, 0))
    full = lambda a: pl.BlockSpec(a.shape, lambda i: (0, 0))
    return pl.pallas_call(
        _layer_out_kernel,
        out_shape=jax.ShapeDtypeStruct((m, D_MODEL), F32),
        grid=(m // tm,),
        in_specs=[row(D_SSD), row(D_ATT), row(D_MODEL),
                  pl.BlockSpec((None, tm, PLE_DIM), lambda i: (layer, i, 0)),
                  full(wo), full(g), full(b), full(wp), full(wg), full(bg)],
        out_specs=row(D_MODEL),
        compiler_params=_cparams(("arbitrary",)),
        name="layer_out",
    )(ys, ya, x, p_all, wo, g, b, wp, wg, bg)


def _head_expand_matrix():
    e = np.zeros((LANES, D_SSD), np.float32)
    for h in range(SSD_HEADS):
        e[h, h * SSD_HEAD_DIM:(h + 1) * SSD_HEAD_DIM] = 1.0
    return e


def _ssd_prompt_kernel(xbc_ref, dt_ref, z_ref, cw_ref, cb_ref, dtb_ref, alog_ref, dsk_ref, nw_ref,
                       exp_ref, expt_ref, y_ref, h_ref, tail_ref):
    cl = SSD_CHUNK
    c = pl.program_id(1)

    @pl.when(c == 0)
    def _():
        tail_ref[...] = jnp.zeros(tail_ref.shape, F32)
        h_ref[...] = jnp.zeros(h_ref.shape, F32)

    x0 = xbc_ref[0]
    tail = tail_ref[...]
    row8 = lax.broadcasted_iota(jnp.int32, (8, CONV_DIM), 0)
    acc = cb_ref[...] + cw_ref[SSD_CONV - 1:SSD_CONV, :] * x0
    for s in range(1, SSD_CONV):
        shifted = pltpu.roll(x0, s, axis=0)
        head = jnp.where(row8 < s, pltpu.roll(tail, s, axis=0), shifted[0:8])
        shifted = jnp.concatenate([head, shifted[8:]], axis=0)
        acc = acc + cw_ref[SSD_CONV - 1 - s:SSD_CONV - s, :] * shifted
    tail_ref[...] = x0[cl - 8:cl]
    act = _silu(acc)
    xs = act[:, 0:D_SSD]
    gn = SSD_STATE
    b_g = [act[:, D_SSD + g * gn:D_SSD + (g + 1) * gn] for g in range(SSD_GROUPS)]
    c_g = [act[:, D_SSD + (SSD_GROUPS + g) * gn:D_SSD + (SSD_GROUPS + g + 1) * gn] for g in range(SSD_GROUPS)]

    dt_h = _softplus(dt_ref[0].T[0:SSD_HEADS, :] + dtb_ref[...])
    dt_t = jnp.concatenate([dt_h, jnp.zeros((LANES - SSD_HEADS, cl), F32)], axis=0)
    dt = dt_t.T
    a_neg = -jnp.exp(alog_ref[...])
    d_a = dt * a_neg
    row = lax.broadcasted_iota(jnp.int32, (cl, cl), 0)
    col = lax.broadcasted_iota(jnp.int32, (cl, cl), 1)
    causal = col <= row
    tril = jnp.where(causal, 1.0, 0.0).astype(BF16)
    acum = _dot_sel_lhs(tril, d_a)
    acum_t = acum.T
    a_last = acum[cl - 1:cl, :]

    e_mat = exp_ref[...]
    e_acum = _dot_sel_rhs(jnp.exp(acum), e_mat)
    e_rest = _dot_sel_rhs(jnp.exp(a_last - acum) * dt, e_mat)
    xw = xs * e_rest
    cd_col = jnp.broadcast_to(jnp.exp(acum_t[:, cl - 1:cl]), (LANES, LANES))
    cd_full = _dot_sel_lhs(expt_ref[...], cd_col)

    lane = lax.broadcasted_iota(jnp.int32, (cl, LANES), 1)
    lo_half = lane < SSD_HEAD_DIM
    heads_per_group = SSD_HEADS // SSD_GROUPS
    gw = heads_per_group * SSD_HEAD_DIM
    y_parts = []
    for g in range(SSD_GROUPS):
        cbf = c_g[g].astype(BF16)
        bbf = b_g[g].astype(BF16)
        cb = _dot_nt(cbf, bbf)
        h_old = h_ref[0, g * gw:(g + 1) * gw, :]
        y_off = _dot_nt(cbf, h_old.astype(BF16))
        for hp in range(heads_per_group // 2):
            x_pair = xs[:, g * gw + hp * LANES:g * gw + (hp + 1) * LANES]
            pair = None
            for s in range(2):
                h = g * heads_per_group + 2 * hp + s
                seg = acum[:, h:h + 1] - acum_t[h:h + 1, :]
                decay = jnp.exp(jnp.where(causal, seg, -jnp.inf))
                w = (cb * decay) * dt_t[h:h + 1, :]
                x_half = jnp.where(lo_half if s == 0 else jnp.logical_not(lo_half), x_pair, 0.0)
                yd = _dot(w.astype(BF16), x_half.astype(BF16))
                pair = yd if pair is None else pair + yd
            y_parts.append(pair + y_off[:, hp * LANES:(hp + 1) * LANES]
                           * e_acum[:, g * gw + hp * LANES:g * gw + (hp + 1) * LANES])
        st = _dot(xw[:, g * gw:(g + 1) * gw].T.astype(BF16), bbf)
        h_ref[0, g * gw:(g + 1) * gw, :] = cd_full[g * gw:(g + 1) * gw, :] * h_old + st

    y = jnp.concatenate(y_parts, axis=1) + dsk_ref[...] * xs
    y = y * _silu(z_ref[0])
    outs = []
    for g in range(SSD_GROUPS):
        yg = y[:, g * gw:(g + 1) * gw]
        ms = jnp.mean(jnp.square(yg), axis=-1, keepdims=True)
        outs.append(yg * lax.rsqrt(ms + NORM_EPS) * nw_ref[:, g * gw:(g + 1) * gw])
    y_ref[0] = jnp.concatenate(outs, axis=1).astype(y_ref.dtype)


def _ssd_prompt(xbc, dt, z, cw, cb, dtb, alog, dsk, nw, e_mat, e_mat_t):
    b, l, _ = xbc.shape
    assert l % SSD_CHUNK == 0
    nc = l // SSD_CHUNK
    tok = lambda w: pl.BlockSpec((1, SSD_CHUNK, w), lambda i, j: (i, j, 0))
    full = lambda a: pl.BlockSpec(a.shape, lambda i, j: (0,) * a.ndim)
    return pl.pallas_call(
        _ssd_prompt_kernel,
        out_shape=(jax.ShapeDtypeStruct((b, l, D_SSD), BF16),
                   jax.ShapeDtypeStruct((b, SSD_HEADS * SSD_HEAD_DIM, SSD_STATE), F32)),
        grid=(b, nc),
        in_specs=[tok(CONV_DIM), tok(DT_PAD), tok(D_SSD), full(cw), full(cb), full(dtb), full(alog),
                  full(dsk), full(nw), full(e_mat), full(e_mat_t)],
        out_specs=(tok(D_SSD),
                   pl.BlockSpec((1, SSD_HEADS * SSD_HEAD_DIM, SSD_STATE), lambda i, j: (i, 0, 0))),
        scratch_shapes=[pltpu.VMEM((8, CONV_DIM), F32)],
        compiler_params=_cparams(("arbitrary", "arbitrary")),
        name="ssd_prompt",
    )(xbc, dt, z, cw, cb, dtb, alog, dsk, nw, e_mat, e_mat_t)


def _lambda_from(lam_ref):
    s1 = jnp.sum(lam_ref[0:1, :] * lam_ref[1:2, :], axis=-1, keepdims=True)
    s2 = jnp.sum(lam_ref[2:3, :] * lam_ref[3:4, :], axis=-1, keepdims=True)
    lam_init = lam_ref[4:5, 0:1]
    return jnp.exp(s1) - jnp.exp(s2) + lam_init, lam_init


def _attn_prompt_kernel(q_ref, k_ref, v_ref, za_ref, lam_ref, sw_ref, o_ref, qst_ref, vt_ref, m_ref, l_ref, acc_ref,
                        sa_ref, sb_ref, *, tq, tk, rs):
    qi = pl.program_id(2)
    nq = pl.num_programs(2) - 1
    nkb = vt_ref.shape[0]

    @pl.when(qi == 0)
    def _():
        for c in range(nkb):
            vt_ref[c] = v_ref[0, c * tk:(c + 1) * tk, :].astype(F32).T.astype(BF16)
        l_ref[...] = jnp.ones(l_ref.shape, F32)
        acc_ref[...] = jnp.zeros(acc_ref.shape, F32)

    qt = q_ref[0].astype(F32).T
    rowd = lax.broadcasted_iota(jnp.int32, (LANES, tq), 0)
    qst_ref[:, 0:tq] = jnp.where(rowd < ATT_HEAD_DIM, qt, 0.0).astype(BF16)
    qst_ref[:, tq:2 * tq] = jnp.where(rowd >= ATT_HEAD_DIM, qt, 0.0).astype(BF16)

    ot = acc_ref[...] * (1.0 / l_ref[...])
    lam, lam_init = _lambda_from(lam_ref)
    od = (ot[:, 0:tq] - lam * ot[:, tq:2 * tq]).T
    ms = jnp.mean(jnp.square(od), axis=-1, keepdims=True)
    on = od * lax.rsqrt(ms + NORM_EPS) * sw_ref[...] * (1.0 - lam_init)
    o_ref[0] = (on * _silu(za_ref[0])).astype(o_ref.dtype)

    m_ref[...] = jnp.full(m_ref.shape, -jnp.inf, F32)
    l_ref[...] = jnp.zeros(l_ref.shape, F32)
    acc_ref[...] = jnp.zeros(acc_ref.shape, F32)
    n_strips = (2 * tq) // rs

    def strips_for(diag_off):
        out = []
        for r in range(n_strips):
            q_lo = (r * rs) % tq
            if diag_off is not None and diag_off > q_lo + rs - 1:
                continue
            out.append((slice(r * rs, (r + 1) * rs), q_lo))
        return out

    def scores_into(buf, ki, diag_off):
        start = pl.multiple_of(ki * tk, tk)
        k = k_ref[0, pl.ds(start, tk), :]
        for sl, _ in strips_for(diag_off):
            buf[:, sl] = _dot(k, qst_ref[:, sl])

    def consume(buf, ki, diag_off):
        vt = vt_ref[ki]
        strips = strips_for(diag_off)
        probs = []
        for sl, q_lo in strips:
            s = buf[:, sl]
            if diag_off is not None:
                kk = lax.broadcasted_iota(jnp.int32, (tk, rs), 0) + diag_off
                qq = lax.broadcasted_iota(jnp.int32, (tk, rs), 1) + q_lo
                s = jnp.where(kk <= qq, s, -jnp.inf)
            m_prev = m_ref[:, sl]
            m_new = jnp.maximum(m_prev, jnp.max(s, axis=0, keepdims=True))
            alpha = jnp.exp2(m_prev - m_new)
            p = jnp.exp2(s - m_new)
            l_ref[:, sl] = alpha * l_ref[:, sl] + jnp.sum(p, axis=0, keepdims=True)
            m_ref[:, sl] = m_new
            probs.append((alpha, p.astype(BF16)))
        for (sl, _), (alpha, pb) in zip(strips, probs):
            acc_ref[:, sl] = alpha * acc_ref[:, sl] + _dot(vt, pb)

    def phase(cur, nxt, ki, diag_cur, diag_next, has_next=True):
        if has_next:
            scores_into(nxt, ki + 1, diag_next)
        consume(cur, ki, diag_cur)

    scores_into(sa_ref, 0, None)

    def body(j, carry):
        phase(sa_ref, sb_ref, 2 * j, None, None)
        phase(sb_ref, sa_ref, 2 * j + 1, None, None)
        return carry

    live = qi < nq
    lax.fori_loop(0, jnp.where(live, qi, 0), body, 0)

    @pl.when(live)
    def _():
        phase(sa_ref, sb_ref, 2 * qi, 0, tk)
        phase(sb_ref, sa_ref, 2 * qi + 1, tk, None, has_next=False)


def _attn_prompt(q, kb, vb, za, lam_blk, sw, tq, tk, rs):
    b, l, _ = q.shape
    assert l % tq == 0 and tq == 2 * tk and tq % rs == 0
    nq = l // tq
    cur = pl.BlockSpec((1, tq, LANES), lambda bi, h, i: (bi, jnp.minimum(i, nq - 1), h))
    prev = pl.BlockSpec((1, tq, LANES), lambda bi, h, i: (bi, jnp.maximum(i - 1, 0), h))
    kspec = pl.BlockSpec((1, l, LANES), lambda bi, h, i: (bi, 0, h))
    full = lambda a: pl.BlockSpec(a.shape, lambda bi, h, i: (0, 0))
    return pl.pallas_call(
        functools.partial(_attn_prompt_kernel, tq=tq, tk=tk, rs=rs),
        out_shape=jax.ShapeDtypeStruct((b, l, D_ATT), BF16),
        grid=(b, ATT_HEADS, nq + 1),
        in_specs=[cur, kspec, kspec, prev, full(lam_blk), full(sw)],
        out_specs=prev,
        scratch_shapes=[pltpu.VMEM((LANES, 2 * tq), BF16), pltpu.VMEM((l // tk, ATT_V_DIM, tk), BF16),
                        pltpu.VMEM((1, 2 * tq), F32), pltpu.VMEM((1, 2 * tq), F32),
                        pltpu.VMEM((ATT_V_DIM, 2 * tq), F32),
                        pltpu.VMEM((tk, 2 * tq), F32), pltpu.VMEM((tk, 2 * tq), F32)],
        compiler_params=_cparams(("arbitrary", "arbitrary", "arbitrary")),
        name="attn_prompt",
    )(q, kb, vb, za, lam_blk, sw)


def _ssd_sample_kernel(xbc_ref, dt_ref, z_ref, sc_ref, h0_ref, cw_ref, cb_ref, dtb_ref, alog_ref, dsk_ref,
                       nw_ref, exp_ref, y_ref, cn_ref, h_ref, *, nb):
    x_new = xbc_ref[...]
    c0, c1, c2 = sc_ref[:, 0, :], sc_ref[:, 1, :], sc_ref[:, 2, :]
    acc = cb_ref[...] + cw_ref[0:1, :] * c0
    acc = acc + cw_ref[1:2, :] * c1
    acc = acc + cw_ref[2:3, :] * c2
    acc = acc + cw_ref[3:4, :] * x_new
    cn_ref[:, 0, :] = c1
    cn_ref[:, 1, :] = c2
    cn_ref[:, 2, :] = x_new
    act = _silu(acc)
    xs = act[:, 0:D_SSD]
    gn = SSD_STATE
    b_g = [act[:, D_SSD + g * gn:D_SSD + (g + 1) * gn] for g in range(SSD_GROUPS)]
    c_g = [act[:, D_SSD + (SSD_GROUPS + g) * gn:D_SSD + (SSD_GROUPS + g + 1) * gn] for g in range(SSD_GROUPS)]

    dt = _softplus(dt_ref[...] + dtb_ref[...])
    d_a = dt * (-jnp.exp(alog_ref[...]))
    e_mat = exp_ref[...]
    dt_e = _dot_sel_rhs(dt, e_mat)
    dec_e = _dot_sel_rhs(jnp.exp(d_a), e_mat)
    xdt = xs * dt_e

    heads_per_group = SSD_HEADS // SSD_GROUPS
    gw = heads_per_group * SSD_HEAD_DIM
    pad = jnp.zeros((LANES - nb, D_SSD), F32)
    xdt_t = jnp.concatenate([xdt, pad], axis=0).T
    dec_t = jnp.concatenate([dec_e, pad], axis=0).T
    rowid = lax.broadcasted_iota(jnp.int32, (nb, gw), 0)
    y_parts = []
    for g in range(SSD_GROUPS):
        cbf = c_g[g].astype(BF16)
        cbdot = jnp.sum(c_g[g] * b_g[g], axis=-1, keepdims=True)
        y_off = jnp.zeros((nb, gw), F32)
        for j in range(nb):
            h_old = h0_ref[j, g * gw:(g + 1) * gw, :]
            yo = _dot_nt(cbf, h_old.astype(BF16))
            y_off = jnp.where(rowid == j, yo, y_off)
            xcol = xdt_t[g * gw:(g + 1) * gw, j:j + 1]
            dcol = dec_t[g * gw:(g + 1) * gw, j:j + 1]
            h_ref[j, g * gw:(g + 1) * gw, :] = dcol * h_old + xcol * b_g[g][j:j + 1, :]
        sl = slice(g * gw, (g + 1) * gw)
        y_parts.append((cbdot * dt_e[:, sl]) * xs[:, sl] + y_off * dec_e[:, sl])
    y = jnp.concatenate(y_parts, axis=1) + dsk_ref[...] * xs
    y = y * _silu(z_ref[...])
    outs = []
    for g in range(SSD_GROUPS):
        yg = y[:, g * gw:(g + 1) * gw]
        ms = jnp.mean(jnp.square(yg), axis=-1, keepdims=True)
        outs.append(yg * lax.rsqrt(ms + NORM_EPS) * nw_ref[:, g * gw:(g + 1) * gw])
    y_ref[...] = jnp.concatenate(outs, axis=1).astype(y_ref.dtype)


def _ssd_sample(xbc, dt, z, state_conv, state_ssm, layer, cw, cb, dtb, alog, dsk, nw, e_mat, nb=8):
    db = xbc.shape[0]
    assert db % nb == 0
    hp = SSD_HEADS * SSD_HEAD_DIM
    ssm = state_ssm.reshape(DEPTH, db, hp, SSD_STATE)
    tok = lambda w: pl.BlockSpec((nb, w), lambda i: (i, 0))
    full = lambda a: pl.BlockSpec(a.shape, lambda i: (0,) * a.ndim)
    return pl.pallas_call(
        functools.partial(_ssd_sample_kernel, nb=nb),
        out_shape=(jax.ShapeDtypeStruct((db, D_SSD), BF16),
                   jax.ShapeDtypeStruct((db, SSD_CONV - 1, CONV_DIM), F32),
                   jax.ShapeDtypeStruct((db, hp, SSD_STATE), F32)),
        grid=(db // nb,),
        in_specs=[tok(CONV_DIM), tok(DT_PAD), tok(D_SSD),
                  pl.BlockSpec((None, nb, SSD_CONV - 1, CONV_DIM), lambda i: (layer, i, 0, 0)),
                  pl.BlockSpec((None, nb, hp, SSD_STATE), lambda i: (layer, i, 0, 0)),
                  full(cw), full(cb), full(dtb), full(alog), full(dsk), full(nw), full(e_mat)],
        out_specs=(tok(D_SSD),
                   pl.BlockSpec((nb, SSD_CONV - 1, CONV_DIM), lambda i: (i, 0, 0)),
                   pl.BlockSpec((nb, hp, SSD_STATE), lambda i: (i, 0, 0))),
        compiler_params=_cparams(("arbitrary",)),
        name="ssd_sample",
    )(xbc, dt, z, state_conv, ssm, cw, cb, dtb, alog, dsk, nw, e_mat)


N_QROWS = 2 * ATT_HEADS


def _attn_sample_kernel(pt_ref, q_ref, kn_ref, vn_ref, za_ref, lam_ref, sw_ref, rexp_ref, *rest, pp):
    k_refs = rest[0:pp]
    v_refs = rest[pp:2 * pp]
    o_ref = rest[2 * pp]
    qsel_ref, m_ref, l_ref, acc_ref = rest[2 * pp + 1:]
    j = pl.program_id(1)
    nj = pl.num_programs(1)

    @pl.when(j == 0)
    def _():
        rowi = lax.broadcasted_iota(jnp.int32, (N_QROWS, QK_DIM), 0)
        coli = lax.broadcasted_iota(jnp.int32, (N_QROWS, QK_DIM), 1)
        own = (coli >> 6) == (2 * (rowi & (ATT_HEADS - 1)) + (rowi >> 3))
        qb = jnp.broadcast_to(q_ref[0].astype(F32), (N_QROWS, QK_DIM))
        qsel_ref[...] = jnp.where(own, qb, 0.0).astype(BF16)
        m_ref[...] = jnp.full(m_ref.shape, -jnp.inf, F32)
        l_ref[...] = jnp.zeros(l_ref.shape, F32)
        acc_ref[...] = jnp.zeros(acc_ref.shape, F32)

    qsel = qsel_ref[...]
    page = k_refs[0].shape[1]
    s = jnp.concatenate([_dot(qsel, k_refs[i][...].astype(BF16)) for i in range(pp)], axis=1)
    m_prev = m_ref[...]
    m_new = jnp.maximum(m_prev, jnp.max(s, axis=-1, keepdims=True))
    alpha = jnp.exp2(m_prev - m_new)
    p = jnp.exp2(s - m_new)
    l_ref[...] = alpha * l_ref[...] + jnp.sum(p, axis=-1, keepdims=True)
    pb = p.astype(BF16)
    p_stack = jnp.concatenate([pb[:, i * page:(i + 1) * page] for i in range(pp)], axis=0)
    p_exp = _dot(p_stack, rexp_ref[...])
    rowe = lax.broadcasted_iota(jnp.int32, p_exp.shape, 0)
    cole = lax.broadcasted_iota(jnp.int32, p_exp.shape, 1)
    keep = (cole & (ATT_HEADS - 1)) == (rowe & (ATT_HEADS - 1))
    p2 = jnp.where(keep, p_exp, 0.0).astype(BF16)
    pv = _dot(p2[0:N_QROWS, :], v_refs[0][...].astype(BF16))
    for i in range(1, pp):
        pv = pv + _dot(p2[i * N_QROWS:(i + 1) * N_QROWS, :], v_refs[i][...].astype(BF16))
    acc_ref[...] = alpha * acc_ref[...] + pv
    m_ref[...] = m_new

    @pl.when(j == nj - 1)
    def _():
        kn = jnp.broadcast_to(kn_ref[0], (N_QROWS, QK_DIM))
        s_new = jnp.sum(qsel_ref[...].astype(F32) * kn, axis=-1, keepdims=True)
        m_p = m_ref[...]
        m_f = jnp.maximum(m_p, s_new)
        a_f = jnp.exp2(m_p - m_f)
        p_new = jnp.exp2(s_new - m_f)
        l_f = a_f * l_ref[...] + p_new
        vn = jnp.concatenate([vn_ref[0], vn_ref[0]], axis=0)
        o = (a_f * acc_ref[...] + p_new * vn) / l_f
        lam, lam_init = _lambda_from(lam_ref)
        od = o[0:ATT_HEADS, :] - lam * o[ATT_HEADS:N_QROWS, :]
        ms = jnp.mean(jnp.square(od), axis=-1, keepdims=True)
        on = od * lax.rsqrt(ms + NORM_EPS) * sw_ref[...] * (1.0 - lam_init)
        o_ref[0] = on * _silu(za_ref[0])


def _attn_sample(page_table, q, k_new, v_new, za, cache_k, cache_v, layer, lam_blk, sw, pp=8):
    db, n_pages = page_table.shape
    assert n_pages % pp == 0
    n_pool, page = cache_k.shape[1], cache_k.shape[2]
    r_np = np.zeros((page, page * ATT_HEADS), np.float32)
    for key in range(page):
        r_np[key, key * ATT_HEADS:(key + 1) * ATT_HEADS] = 1.0
    rexp = jnp.asarray(r_np, BF16)
    ck = jnp.transpose(cache_k, (0, 1, 3, 4, 5, 2)).reshape(DEPTH, n_pool, QK_DIM, page)
    cv = cache_v.reshape(DEPTH, n_pool, page * ATT_HEADS, ATT_V_DIM)
    tok = lambda w: pl.BlockSpec((1, 1, w), lambda b, j, pt: (b, 0, 0))
    tokh = pl.BlockSpec((1, ATT_HEADS, ATT_V_DIM), lambda b, j, pt: (b, 0, 0))
    full = lambda a: pl.BlockSpec(a.shape, lambda b, j, pt: (0, 0))

    def page_spec(i, rows, w):
        return pl.BlockSpec((None, None, rows, w), lambda b, j, pt: (layer, pt[b, j * pp + i], 0, 0))

    grid_spec = pltpu.PrefetchScalarGridSpec(
        num_scalar_prefetch=1,
        grid=(db, n_pages // pp),
        in_specs=[tok(QK_DIM), tok(QK_DIM), tokh, tokh, full(lam_blk), full(sw), full(rexp)]
        + [page_spec(i, QK_DIM, page) for i in range(pp)]
        + [page_spec(i, page * ATT_HEADS, ATT_V_DIM) for i in range(pp)],
        out_specs=tokh,
        scratch_shapes=[pltpu.VMEM((N_QROWS, QK_DIM), BF16), pltpu.VMEM((N_QROWS, 1), F32),
                        pltpu.VMEM((N_QROWS, 1), F32), pltpu.VMEM((N_QROWS, ATT_V_DIM), F32)],
    )
    r3 = lambda a: a.reshape(db, 1, a.shape[-1])
    rh = lambda a: a.reshape(db, ATT_HEADS, ATT_V_DIM)
    out = pl.pallas_call(
        functools.partial(_attn_sample_kernel, pp=pp),
        out_shape=jax.ShapeDtypeStruct((db, ATT_HEADS, ATT_V_DIM), F32),
        grid_spec=grid_spec,
        compiler_params=_cparams(("arbitrary", "arbitrary")),
        name="attn_sample",
    )(page_table, r3(q), r3(k_new), rh(v_new), rh(za), lam_blk, sw, rexp, *([ck] * pp), *([cv] * pp))
    return out.reshape(db, D_ATT)


def _pad_lanes(v, width=LANES):
    v = v.reshape(1, -1)
    return jnp.pad(v, ((0, 0), (0, width - v.shape[1])))


def _pick_tile(m, pref):
    t = min(pref, m)
    while m % t:
        t //= 2
    return t


def kernel(x_prompt, x_sample, cache_k, cache_v, state_ssm, state_conv, page_table, p_prompt, p_sample,
           w_in, conv_w, conv_b, dt_bias, a_log, d_skip, ssd_norm_w, lam_q1, lam_k1, lam_q2, lam_k2,
           subln_w, w_out, ln_g, ln_b, w_ple, w_ple_gate, b_ple_gate):
    b, l, _ = x_prompt.shape
    db = x_sample.shape[0]
    mp = b * l
    xp = x_prompt.reshape(mp, D_MODEL)
    xs = x_sample.reshape(db, D_MODEL)
    e_np = _head_expand_matrix()
    e_mat = jnp.asarray(e_np, BF16)
    e_mat_t = jnp.asarray(e_np.T, BF16)
    ts_in = _pick_tile(l, 256 // b)
    tm_out = _pick_tile(mp, 512)
    tq = _pick_tile(l, 1024)
    tk = tq // 2
    rs = _pick_tile(tk, 256)
    row = lambda v: v.reshape(1, -1)
    dt_lo = D_SSD + CONV_DIM
    w_all = jnp.concatenate(
        [w_in[:, :, :dt_lo], w_in[:, :, dt_lo + SSD_HEADS:], w_in[:, :, dt_lo:dt_lo + SSD_HEADS],
         jnp.zeros((DEPTH, D_MODEL, DT_PAD - SSD_HEADS), w_in.dtype)], axis=2).astype(BF16)
    pp_all = p_prompt.reshape(DEPTH, mp, PLE_DIM)
    ps_all = p_sample.reshape(DEPTH, db, PLE_DIM)

    kt_all, v_all = None, None
    hp, cp, ksl, vsl, hsl, csl = [], [], [], [], [], []
    for layer in range(DEPTH):
        lam_init = 0.8 - 0.6 * math.exp(-0.3 * layer)
        cw, cb = conv_w[layer], row(conv_b[layer])
        dtb, alog = _pad_lanes(dt_bias[layer]), _pad_lanes(a_log[layer])
        dsk = row(jnp.repeat(d_skip[layer], SSD_HEAD_DIM))
        nw = row(ssd_norm_w[layer])
        lam_blk = jnp.concatenate(
            [_pad_lanes(lam_q1[layer]), _pad_lanes(lam_k1[layer]), _pad_lanes(lam_q2[layer]),
             _pad_lanes(lam_k2[layer]), jnp.full((1, LANES), lam_init, F32), jnp.zeros((3, LANES), F32)], axis=0)
        sw = row(subln_w[layer])
        wo = w_out[layer].astype(BF16)
        wp = w_ple[layer].astype(BF16)
        wg = w_ple_gate[layer].astype(BF16)
        g, bb, bg = row(ln_g[layer]), row(ln_b[layer]), row(b_ple_gate[layer])

        z, xbc, dt, q, za, kb, vb, kt_all, v_all = _in_proj_prompt(
            xp.reshape(b, l, D_MODEL), w_all, layer, kt_all, v_all, ts_in)
        y_ssd, h_new = _ssd_prompt(xbc, dt, z, cw, cb, dt_bias[layer].reshape(SSD_HEADS, 1), alog, dsk, nw, e_mat, e_mat_t)
        y_att = _attn_prompt(q, kb, vb, za, lam_blk, sw, tq, tk, rs)
        xp = _layer_out(y_ssd.reshape(mp, D_SSD), y_att.reshape(mp, D_ATT), xp,
                        pp_all, layer, wo, g, bb, wp, wg, bg, tm_out)
        hp.append(h_new.reshape(b, SSD_HEADS, SSD_HEAD_DIM, SSD_STATE))
        cp.append(xbc[:, l - (SSD_CONV - 1):, :])

        z, xbc, dt, q, k, v, za, _, _ = _in_proj(xs, w_all, layer, db)
        y_ssd, conv_new, h_new = _ssd_sample(xbc, dt, z, state_conv, state_ssm, layer,
                                             cw, cb, dtb, alog, dsk, nw, e_mat)
        y_att = _attn_sample(page_table, q, k, v, za, cache_k, cache_v, layer, lam_blk, sw)
        xs = _layer_out(y_ssd, y_att, xs, ps_all, layer, wo, g, bb, wp, wg, bg, db)
        ksl.append(k.reshape(db, 1, ATT_HEADS, 2, ATT_HEAD_DIM))
        vsl.append(v.reshape(db, 1, ATT_HEADS, ATT_V_DIM))
        hsl.append(h_new.reshape(db, SSD_HEADS, SSD_HEAD_DIM, SSD_STATE))
        csl.append(conv_new)

    k_prompt = kt_all.reshape(DEPTH, b, ATT_HEADS, 2, ATT_HEAD_DIM, l).transpose(0, 1, 5, 2, 3, 4)
    v_prompt = v_all.reshape(DEPTH, b, l, ATT_HEADS, ATT_V_DIM)
    return (xp.reshape(b, l, D_MODEL), xs.reshape(db, 1, D_MODEL), k_prompt, v_prompt,
            jnp.stack(hp), jnp.stack(cp), jnp.stack(ksl), jnp.stack(vsl), jnp.stack(hsl), jnp.stack(csl))
```

```python
import functools
import math

import numpy as np
import jax
import jax.numpy as jnp
from jax import lax
from jax.experimental import pallas as pl
from jax.experimental.pallas import tpu as pltpu

F32 = jnp.float32
BF16 = jnp.bfloat16

D_MODEL = 1024
DEPTH = 4
D_SSD = 1024
SSD_HEAD_DIM = 64
SSD_HEADS = 16
SSD_STATE = 128
SSD_GROUPS = 2
SSD_CONV = 4
SSD_CHUNK = 128
CONV_DIM = D_SSD + 2 * SSD_GROUPS * SSD_STATE
ATT_HEAD_DIM = 64
ATT_V_DIM = 128
ATT_HEADS = 8
D_ATT = 1024
QK_DIM = 1024
ATT_SCALE = ATT_HEAD_DIM ** -0.5
Q_SCALE = ATT_SCALE * math.log2(math.e)
PLE_DIM = 256
DEEPNORM_ALPHA = (2 * DEPTH) ** 0.25
NORM_EPS = 1e-5
LANES = 128
DT_PAD = LANES
C_Z, C_XBC, C_Q, C_K, C_V, C_ZA, C_DT, C_END = 0, 1024, 2560, 3584, 4608, 5632, 6656, 6784
VMEM_LIMIT = 56 * 1024 * 1024


def _cparams(sem):
    return pltpu.CompilerParams(dimension_semantics=sem, vmem_limit_bytes=VMEM_LIMIT)


def _silu(x):
    return x * jax.nn.sigmoid(x)


def _softplus(x):
    return jnp.maximum(x, 0.0) + jnp.log1p(jnp.exp(-jnp.abs(x)))


def _split3(a):
    hi = a.astype(BF16)
    r1 = a - hi.astype(F32)
    mid = r1.astype(BF16)
    lo = (r1 - mid.astype(F32)).astype(BF16)
    return hi, mid, lo


def _dot(a, b):
    return jnp.dot(a, b, preferred_element_type=F32)


def _dot_nt(a, b):
    return lax.dot_general(a, b, (((1,), (1,)), ((), ())), preferred_element_type=F32)


def _dot_sel_rhs(a, sel):
    hi, mid, lo = _split3(a)
    return (_dot(hi, sel) + _dot(mid, sel)) + _dot(lo, sel)


def _dot_sel_lhs(sel, b):
    hi, mid, lo = _split3(b)
    return (_dot(sel, hi) + _dot(sel, mid)) + _dot(sel, lo)


def _in_proj_kernel(x_ref, w_ref, z_ref, xbc_ref, dt_ref, q_ref, k_ref, v_ref, za_ref, kb_ref, vb_ref):
    xb = x_ref[...].astype(BF16)

    def mm(lo, hi):
        return _dot(xb, w_ref[:, lo:hi])

    z_ref[...] = mm(C_Z, C_XBC)
    xbc_ref[...] = mm(C_XBC, C_Q)
    q_ref[...] = (mm(C_Q, C_K) * Q_SCALE).astype(BF16)
    k = mm(C_K, C_V)
    k_ref[...] = k
    kb_ref[...] = k.astype(BF16)
    v = mm(C_V, C_ZA)
    v_ref[...] = v
    vb_ref[...] = v.astype(BF16)
    za_ref[...] = mm(C_ZA, C_DT)
    dt_ref[...] = mm(C_DT, C_END)


def _in_proj(x, w_all, layer, tm):
    m = x.shape[0]
    assert m % tm == 0
    widths = (1024, CONV_DIM, DT_PAD, 1024, 1024, 1024, 1024, 1024, 1024)
    dtypes = (F32, F32, F32, BF16, F32, F32, F32, BF16, BF16)
    out_shape = tuple(jax.ShapeDtypeStruct((m, w), d) for w, d in zip(widths, dtypes))
    out_specs = tuple(pl.BlockSpec((tm, w), lambda i: (i, 0)) for w in widths)
    return pl.pallas_call(
        _in_proj_kernel,
        out_shape=out_shape,
        grid=(m // tm,),
        in_specs=[pl.BlockSpec((tm, D_MODEL), lambda i: (i, 0)),
                  pl.BlockSpec((None, D_MODEL, C_END), lambda i: (layer, 0, 0))],
        out_specs=out_specs,
        compiler_params=_cparams(("arbitrary",)),
        name="in_proj",
    )(x, w_all)


def _in_proj_prompt_kernel(x_ref, w_ref, kt_in_ref, v_in_ref, z_ref, xbc_ref, dt_ref, q_ref, za_ref, kb_ref, vb_ref,
                           kt_ref, v_ref):
    del kt_in_ref, v_in_ref
    nb, ts, _ = x_ref.shape
    xb = x_ref[...].reshape(nb * ts, D_MODEL).astype(BF16)

    def mm(lo, hi):
        return _dot(xb, w_ref[:, lo:hi])

    def put(ref, val):
        ref[...] = val.reshape(nb, ts, val.shape[-1]).astype(ref.dtype)

    put(z_ref, mm(C_Z, C_XBC))
    put(xbc_ref, mm(C_XBC, C_Q))
    put(q_ref, mm(C_Q, C_K) * Q_SCALE)
    k = mm(C_K, C_V)
    put(kb_ref, k)
    for bi in range(nb):
        kt_ref[bi] = k[bi * ts:(bi + 1) * ts, :].T
    v = mm(C_V, C_ZA)
    put(v_ref, v)
    put(vb_ref, v)
    put(za_ref, mm(C_ZA, C_DT))
    put(dt_ref, mm(C_DT, C_END))


def _in_proj_prompt(x, w_all, layer, kt_all, v_all, ts):
    b, l, _ = x.shape
    assert l % ts == 0
    widths = (1024, CONV_DIM, DT_PAD, 1024, 1024, 1024, 1024)
    dtypes = (F32, F32, F32, BF16, F32, BF16, BF16)
    tok = lambda w: pl.BlockSpec((b, ts, w), lambda i: (0, i, 0))
    out_shape = tuple(jax.ShapeDtypeStruct((b, l, w), d) for w, d in zip(widths, dtypes)) + (
        jax.ShapeDtypeStruct((DEPTH, b, QK_DIM, l), F32), jax.ShapeDtypeStruct((DEPTH, b, l, D_ATT), F32))
    out_specs = tuple(tok(w) for w in widths) + (
        pl.BlockSpec((None, b, QK_DIM, ts), lambda i: (layer, 0, 0, i)),
        pl.BlockSpec((None, b, ts, D_ATT), lambda i: (layer, 0, i, 0)))
    in_specs = [tok(D_MODEL), pl.BlockSpec((None, D_MODEL, C_END), lambda i: (layer, 0, 0)),
                pl.BlockSpec(memory_space=pl.ANY), pl.BlockSpec(memory_space=pl.ANY)]
    return pl.pallas_call(
        _in_proj_prompt_kernel,
        out_shape=out_shape,
        grid=(l // ts,),
        in_specs=in_specs,
        out_specs=out_specs,
        input_output_aliases={2: len(widths), 3: len(widths) + 1},
        compiler_params=_cparams(("arbitrary",)),
        name="in_proj_prompt",
    )(x, w_all, kt_all, v_all)


def _layer_out_kernel(ys_ref, ya_ref, x_ref, p_ref, wo_ref, g_ref, b_ref, wp_ref, wg_ref, bg_ref, o_ref):
    y = _dot(ys_ref[...].astype(BF16), wo_ref[0:D_SSD, :]) + _dot(ya_ref[...].astype(BF16), wo_ref[D_SSD:, :])
    t = DEEPNORM_ALPHA * x_ref[...] + y
    mu = jnp.mean(t, axis=-1, keepdims=True)
    var = jnp.mean(jnp.square(t - mu), axis=-1, keepdims=True)
    h = ((t - mu) * lax.rsqrt(var + NORM_EPS)) * g_ref[...] + b_ref[...]
    gate = jax.nn.sigmoid(_dot(h.astype(BF16), wg_ref[...]) + bg_ref[...])
    o_ref[...] = h + gate * _dot(p_ref[...].astype(BF16), wp_ref[...])


def _layer_out(ys, ya, x, p_all, layer, wo, g, b, wp, wg, bg, tm):
    m = x.shape[0]
    assert m % tm == 0
    row = lambda w: pl.BlockSpec((tm, w), lambda i: (i, 0))
    full = lambda a: pl.BlockSpec(a.shape, lambda i: (0, 0))
    return pl.pallas_call(
        _layer_out_kernel,
        out_shape=jax.ShapeDtypeStruct((m, D_MODEL), F32),
        grid=(m // tm,),
        in_specs=[row(D_SSD), row(D_ATT), row(D_MODEL),
                  pl.BlockSpec((None, tm, PLE_DIM), lambda i: (layer, i, 0)),
                  full(wo), full(g), full(b), full(wp), full(wg), full(bg)],
        out_specs=row(D_MODEL),
        compiler_params=_cparams(("arbitrary",)),
        name="layer_out",
    )(ys, ya, x, p_all, wo, g, b, wp, wg, bg)


def _head_expand_matrix():
    e = np.zeros((LANES, D_SSD), np.float32)
    for h in range(SSD_HEADS):
        e[h, h * SSD_HEAD_DIM:(h + 1) * SSD_HEAD_DIM] = 1.0
    return e


def _ssd_prompt_kernel(xbc_ref, dt_ref, z_ref, cw_ref, cb_ref, dtb_ref, alog_ref, dsk_ref, nw_ref,
                       exp_ref, expt_ref, y_ref, h_ref, tail_ref):
    cl = SSD_CHUNK
    c = pl.program_id(1)

    @pl.when(c == 0)
    def _():
        tail_ref[...] = jnp.zeros(tail_ref.shape, F32)
        h_ref[...] = jnp.zeros(h_ref.shape, F32)

    x0 = xbc_ref[0]
    tail = tail_ref[...]
    row8 = lax.broadcasted_iota(jnp.int32, (8, CONV_DIM), 0)
    acc = cb_ref[...] + cw_ref[SSD_CONV - 1:SSD_CONV, :] * x0
    for s in range(1, SSD_CONV):
        shifted = pltpu.roll(x0, s, axis=0)
        head = jnp.where(row8 < s, pltpu.roll(tail, s, axis=0), shifted[0:8])
        shifted = jnp.concatenate([head, shifted[8:]], axis=0)
        acc = acc + cw_ref[SSD_CONV - 1 - s:SSD_CONV - s, :] * shifted
    tail_ref[...] = x0[cl - 8:cl]
    act = _silu(acc)
    xs = act[:, 0:D_SSD]
    gn = SSD_STATE
    b_g = [act[:, D_SSD + g * gn:D_SSD + (g + 1) * gn] for g in range(SSD_GROUPS)]
    c_g = [act[:, D_SSD + (SSD_GROUPS + g) * gn:D_SSD + (SSD_GROUPS + g + 1) * gn] for g in range(SSD_GROUPS)]

    dt_h = _softplus(dt_ref[0].T[0:SSD_HEADS, :] + dtb_ref[...])
    dt_t = jnp.concatenate([dt_h, jnp.zeros((LANES - SSD_HEADS, cl), F32)], axis=0)
    dt = dt_t.T
    a_neg = -jnp.exp(alog_ref[...])
    d_a = dt * a_neg
    row = lax.broadcasted_iota(jnp.int32, (cl, cl), 0)
    col = lax.broadcasted_iota(jnp.int32, (cl, cl), 1)
    causal = col <= row
    tril = jnp.where(causal, 1.0, 0.0).astype(BF16)
    acum = _dot_sel_lhs(tril, d_a)
    acum_t = acum.T
    a_last = acum[cl - 1:cl, :]

    e_mat = exp_ref[...]
    e_acum = _dot_sel_rhs(jnp.exp(acum), e_mat)
    e_rest = _dot_sel_rhs(jnp.exp(a_last - acum) * dt, e_mat)
    xw = xs * e_rest
    cd_col = jnp.broadcast_to(jnp.exp(acum_t[:, cl - 1:cl]), (LANES, LANES))
    cd_full = _dot_sel_lhs(expt_ref[...], cd_col)

    lane = lax.broadcasted_iota(jnp.int32, (cl, LANES), 1)
    lo_half = lane < SSD_HEAD_DIM
    heads_per_group = SSD_HEADS // SSD_GROUPS
    gw = heads_per_group * SSD_HEAD_DIM
    y_parts = []
    for g in range(SSD_GROUPS):
        cbf = c_g[g].astype(BF16)
        bbf = b_g[g].astype(BF16)
        cb = _dot_nt(cbf, bbf)
        h_old = h_ref[0, g * gw:(g + 1) * gw, :]
        y_off = _dot_nt(cbf, h_old.astype(BF16))
        for hp in range(heads_per_group // 2):
            x_pair = xs[:, g * gw + hp * LANES:g * gw + (hp + 1) * LANES]
            pair = None
            for s in range(2):
                h = g * heads_per_group + 2 * hp + s
                seg = acum[:, h:h + 1] - acum_t[h:h + 1, :]
                decay = jnp.exp(jnp.where(causal, seg, -jnp.inf))
                w = (cb * decay) * dt_t[h:h + 1, :]
                x_half = jnp.where(lo_half if s == 0 else jnp.logical_not(lo_half), x_pair, 0.0)
                yd = _dot(w.astype(BF16), x_half.astype(BF16))
                pair = yd if pair is None else pair + yd
            y_parts.append(pair + y_off[:, hp * LANES:(hp + 1) * LANES]
                           * e_acum[:, g * gw + hp * LANES:g * gw + (hp + 1) * LANES])
        st = _dot(xw[:, g * gw:(g + 1) * gw].T.astype(BF16), bbf)
        h_ref[0, g * gw:(g + 1) * gw, :] = cd_full[g * gw:(g + 1) * gw, :] * h_old + st

    y = jnp.concatenate(y_parts, axis=1) + dsk_ref[...] * xs
    y = y * _silu(z_ref[0])
    outs = []
    for g in range(SSD_GROUPS):
        yg = y[:, g * gw:(g + 1) * gw]
        ms = jnp.mean(jnp.square(yg), axis=-1, keepdims=True)
        outs.append(yg * lax.rsqrt(ms + NORM_EPS) * nw_ref[:, g * gw:(g + 1) * gw])
    y_ref[0] = jnp.concatenate(outs, axis=1).astype(y_ref.dtype)


def _ssd_prompt(xbc, dt, z, cw, cb, dtb, alog, dsk, nw, e_mat, e_mat_t):
    b, l, _ = xbc.shape
    assert l % SSD_CHUNK == 0
    nc = l // SSD_CHUNK
    tok = lambda w: pl.BlockSpec((1, SSD_CHUNK, w), lambda i, j: (i, j, 0))
    full = lambda a: pl.BlockSpec(a.shape, lambda i, j: (0,) * a.ndim)
    return pl.pallas_call(
        _ssd_prompt_kernel,
        out_shape=(jax.ShapeDtypeStruct((b, l, D_SSD), BF16),
                   jax.ShapeDtypeStruct((b, SSD_HEADS * SSD_HEAD_DIM, SSD_STATE), F32)),
        grid=(b, nc),
        in_specs=[tok(CONV_DIM), tok(DT_PAD), tok(D_SSD), full(cw), full(cb), full(dtb), full(alog),
                  full(dsk), full(nw), full(e_mat), full(e_mat_t)],
        out_specs=(tok(D_SSD),
                   pl.BlockSpec((1, SSD_HEADS * SSD_HEAD_DIM, SSD_STATE), lambda i, j: (i, 0, 0))),
        scratch_shapes=[pltpu.VMEM((8, CONV_DIM), F32)],
        compiler_params=_cparams(("arbitrary", "arbitrary")),
        name="ssd_prompt",
    )(xbc, dt, z, cw, cb, dtb, alog, dsk, nw, e_mat, e_mat_t)


def _lambda_from(lam_ref):
    s1 = jnp.sum(lam_ref[0:1, :] * lam_ref[1:2, :], axis=-1, keepdims=True)
    s2 = jnp.sum(lam_ref[2:3, :] * lam_ref[3:4, :], axis=-1, keepdims=True)
    lam_init = lam_ref[4:5, 0:1]
    return jnp.exp(s1) - jnp.exp(s2) + lam_init, lam_init


def _attn_prompt_kernel(q_ref, k_ref, v_ref, za_ref, lam_ref, sw_ref, o_ref, qst_ref, vt_ref, m_ref, l_ref, acc_ref,
                        sa_ref, sb_ref, *, tq, tk, rs):
    qi = pl.program_id(2)
    nkb = vt_ref.shape[0]

    @pl.when(qi == 0)
    def _():
        for c in range(nkb):
            vt_ref[c] = v_ref[0, c * tk:(c + 1) * tk, :].astype(F32).T.astype(BF16)

    qt = q_ref[0].astype(F32).T
    rowd = lax.broadcasted_iota(jnp.int32, (LANES, tq), 0)
    qst_ref[:, 0:tq] = jnp.where(rowd < ATT_HEAD_DIM, qt, 0.0).astype(BF16)
    qst_ref[:, tq:2 * tq] = jnp.where(rowd >= ATT_HEAD_DIM, qt, 0.0).astype(BF16)
    m_ref[...] = jnp.full(m_ref.shape, -jnp.inf, F32)
    l_ref[...] = jnp.zeros(l_ref.shape, F32)
    acc_ref[...] = jnp.zeros(acc_ref.shape, F32)
    n_strips = (2 * tq) // rs

    def strips_for(diag_off):
        out = []
        for r in range(n_strips):
            q_lo = (r * rs) % tq
            if diag_off is not None and diag_off > q_lo + rs - 1:
                continue
            out.append((slice(r * rs, (r + 1) * rs), q_lo))
        return out

    def scores_into(buf, ki, diag_off):
        start = pl.multiple_of(ki * tk, tk)
        k = k_ref[0, pl.ds(start, tk), :]
        for sl, _ in strips_for(diag_off):
            buf[:, sl] = _dot(k, qst_ref[:, sl])

    def consume(buf, ki, diag_off):
        vt = vt_ref[ki]
        strips = strips_for(diag_off)
        probs = []
        for sl, q_lo in strips:
            s = buf[:, sl]
            if diag_off is not None:
                kk = lax.broadcasted_iota(jnp.int32, (tk, rs), 0) + diag_off
                qq = lax.broadcasted_iota(jnp.int32, (tk, rs), 1) + q_lo
                s = jnp.where(kk <= qq, s, -jnp.inf)
            m_prev = m_ref[:, sl]
            m_new = jnp.maximum(m_prev, jnp.max(s, axis=0, keepdims=True))
            alpha = jnp.exp2(m_prev - m_new)
            p = jnp.exp2(s - m_new)
            l_ref[:, sl] = alpha * l_ref[:, sl] + jnp.sum(p, axis=0, keepdims=True)
            m_ref[:, sl] = m_new
            probs.append((alpha, p.astype(BF16)))
        for (sl, _), (alpha, pb) in zip(strips, probs):
            acc_ref[:, sl] = alpha * acc_ref[:, sl] + _dot(vt, pb)

    def phase(cur, nxt, ki, diag_cur, diag_next, has_next=True):
        if has_next:
            scores_into(nxt, ki + 1, diag_next)
        consume(cur, ki, diag_cur)

    scores_into(sa_ref, 0, None)

    def body(j, carry):
        phase(sa_ref, sb_ref, 2 * j, None, None)
        phase(sb_ref, sa_ref, 2 * j + 1, None, None)
        return carry

    lax.fori_loop(0, qi, body, 0)
    phase(sa_ref, sb_ref, 2 * qi, 0, tk)
    phase(sb_ref, sa_ref, 2 * qi + 1, tk, None, has_next=False)

    ot = acc_ref[...] * (1.0 / l_ref[...])
    lam, lam_init = _lambda_from(lam_ref)
    od = (ot[:, 0:tq] - lam * ot[:, tq:2 * tq]).T
    ms = jnp.mean(jnp.square(od), axis=-1, keepdims=True)
    on = od * lax.rsqrt(ms + NORM_EPS) * sw_ref[...] * (1.0 - lam_init)
    o_ref[0] = (on * _silu(za_ref[0])).astype(o_ref.dtype)


def _attn_prompt(q, kb, vb, za, lam_blk, sw, tq, tk, rs):
    b, l, _ = q.shape
    assert l % tq == 0 and tq == 2 * tk and tq % rs == 0
    nq = l // tq
    qspec = pl.BlockSpec((1, tq, LANES), lambda bi, h, i: (bi, i, h))
    kspec = pl.BlockSpec((1, l, LANES), lambda bi, h, i: (bi, 0, h))
    full = lambda a: pl.BlockSpec(a.shape, lambda bi, h, i: (0, 0))
    return pl.pallas_call(
        functools.partial(_attn_prompt_kernel, tq=tq, tk=tk, rs=rs),
        out_shape=jax.ShapeDtypeStruct((b, l, D_ATT), BF16),
        grid=(b, ATT_HEADS, nq),
        in_specs=[qspec, kspec, kspec, qspec, full(lam_blk), full(sw)],
        out_specs=qspec,
        scratch_shapes=[pltpu.VMEM((LANES, 2 * tq), BF16), pltpu.VMEM((l // tk, ATT_V_DIM, tk), BF16),
                        pltpu.VMEM((1, 2 * tq), F32), pltpu.VMEM((1, 2 * tq), F32),
                        pltpu.VMEM((ATT_V_DIM, 2 * tq), F32),
                        pltpu.VMEM((tk, 2 * tq), F32), pltpu.VMEM((tk, 2 * tq), F32)],
        compiler_params=_cparams(("arbitrary", "arbitrary", "arbitrary")),
        name="attn_prompt",
    )(q, kb, vb, za, lam_blk, sw)


def _ssd_sample_kernel(xbc_ref, dt_ref, z_ref, sc_ref, h0_ref, cw_ref, cb_ref, dtb_ref, alog_ref, dsk_ref,
                       nw_ref, exp_ref, y_ref, cn_ref, h_ref, *, nb):
    x_new = xbc_ref[...]
    c0, c1, c2 = sc_ref[:, 0, :], sc_ref[:, 1, :], sc_ref[:, 2, :]
    acc = cb_ref[...] + cw_ref[0:1, :] * c0
    acc = acc + cw_ref[1:2, :] * c1
    acc = acc + cw_ref[2:3, :] * c2
    acc = acc + cw_ref[3:4, :] * x_new
    cn_ref[:, 0, :] = c1
    cn_ref[:, 1, :] = c2
    cn_ref[:, 2, :] = x_new
    act = _silu(acc)
    xs = act[:, 0:D_SSD]
    gn = SSD_STATE
    b_g = [act[:, D_SSD + g * gn:D_SSD + (g + 1) * gn] for g in range(SSD_GROUPS)]
    c_g = [act[:, D_SSD + (SSD_GROUPS + g) * gn:D_SSD + (SSD_GROUPS + g + 1) * gn] for g in range(SSD_GROUPS)]

    dt = _softplus(dt_ref[...] + dtb_ref[...])
    d_a = dt * (-jnp.exp(alog_ref[...]))
    e_mat = exp_ref[...]
    dt_e = _dot_sel_rhs(dt, e_mat)
    dec_e = _dot_sel_rhs(jnp.exp(d_a), e_mat)
    xdt = xs * dt_e

    heads_per_group = SSD_HEADS // SSD_GROUPS
    gw = heads_per_group * SSD_HEAD_DIM
    pad = jnp.zeros((LANES - nb, D_SSD), F32)
    xdt_t = jnp.concatenate([xdt, pad], axis=0).T
    dec_t = jnp.concatenate([dec_e, pad], axis=0).T
    rowid = lax.broadcasted_iota(jnp.int32, (nb, gw), 0)
    y_parts = []
    for g in range(SSD_GROUPS):
        cbf = c_g[g].astype(BF16)
        cbdot = jnp.sum(c_g[g] * b_g[g], axis=-1, keepdims=True)
        y_off = jnp.zeros((nb, gw), F32)
        for j in range(nb):
            h_old = h0_ref[j, g * gw:(g + 1) * gw, :]
            yo = _dot_nt(cbf, h_old.astype(BF16))
            y_off = jnp.where(rowid == j, yo, y_off)
            xcol = xdt_t[g * gw:(g + 1) * gw, j:j + 1]
            dcol = dec_t[g * gw:(g + 1) * gw, j:j + 1]
            h_ref[j, g * gw:(g + 1) * gw, :] = dcol * h_old + xcol * b_g[g][j:j + 1, :]
        sl = slice(g * gw, (g + 1) * gw)
        y_parts.append((cbdot * dt_e[:, sl]) * xs[:, sl] + y_off * dec_e[:, sl])
    y = jnp.concatenate(y_parts, axis=1) + dsk_ref[...] * xs
    y = y * _silu(z_ref[...])
    outs = []
    for g in range(SSD_GROUPS):
        yg = y[:, g * gw:(g + 1) * gw]
        ms = jnp.mean(jnp.square(yg), axis=-1, keepdims=True)
        outs.append(yg * lax.rsqrt(ms + NORM_EPS) * nw_ref[:, g * gw:(g + 1) * gw])
    y_ref[...] = jnp.concatenate(outs, axis=1).astype(y_ref.dtype)


def _ssd_sample(xbc, dt, z, state_conv, state_ssm, layer, cw, cb, dtb, alog, dsk, nw, e_mat, nb=8):
    db = xbc.shape[0]
    assert db % nb == 0
    hp = SSD_HEADS * SSD_HEAD_DIM
    ssm = state_ssm.reshape(DEPTH, db, hp, SSD_STATE)
    tok = lambda w: pl.BlockSpec((nb, w), lambda i: (i, 0))
    full = lambda a: pl.BlockSpec(a.shape, lambda i: (0,) * a.ndim)
    return pl.pallas_call(
        functools.partial(_ssd_sample_kernel, nb=nb),
        out_shape=(jax.ShapeDtypeStruct((db, D_SSD), BF16),
                   jax.ShapeDtypeStruct((db, SSD_CONV - 1, CONV_DIM), F32),
                   jax.ShapeDtypeStruct((db, hp, SSD_STATE), F32)),
        grid=(db // nb,),
        in_specs=[tok(CONV_DIM), tok(DT_PAD), tok(D_SSD),
                  pl.BlockSpec((None, nb, SSD_CONV - 1, CONV_DIM), lambda i: (layer, i, 0, 0)),
                  pl.BlockSpec((None, nb, hp, SSD_STATE), lambda i: (layer, i, 0, 0)),
                  full(cw), full(cb), full(dtb), full(alog), full(dsk), full(nw), full(e_mat)],
        out_specs=(tok(D_SSD),
                   pl.BlockSpec((nb, SSD_CONV - 1, CONV_DIM), lambda i: (i, 0, 0)),
                   pl.BlockSpec((nb, hp, SSD_STATE), lambda i: (i, 0, 0))),
        compiler_params=_cparams(("arbitrary",)),
        name="ssd_sample",
    )(xbc, dt, z, state_conv, ssm, cw, cb, dtb, alog, dsk, nw, e_mat)


N_QROWS = 2 * ATT_HEADS


def _attn_sample_kernel(pt_ref, q_ref, kn_ref, vn_ref, za_ref, lam_ref, sw_ref, rexp_ref, *rest, pp):
    k_refs = rest[0:pp]
    v_refs = rest[pp:2 * pp]
    o_ref = rest[2 * pp]
    qsel_ref, m_ref, l_ref, acc_ref = rest[2 * pp + 1:]
    j = pl.program_id(1)
    nj = pl.num_programs(1)

    @pl.when(j == 0)
    def _():
        rowi = lax.broadcasted_iota(jnp.int32, (N_QROWS, QK_DIM), 0)
        coli = lax.broadcasted_iota(jnp.int32, (N_QROWS, QK_DIM), 1)
        own = (coli >> 6) == (2 * (rowi & (ATT_HEADS - 1)) + (rowi >> 3))
        qb = jnp.broadcast_to(q_ref[0].astype(F32), (N_QROWS, QK_DIM))
        qsel_ref[...] = jnp.where(own, qb, 0.0).astype(BF16)
        m_ref[...] = jnp.full(m_ref.shape, -jnp.inf, F32)
        l_ref[...] = jnp.zeros(l_ref.shape, F32)
        acc_ref[...] = jnp.zeros(acc_ref.shape, F32)

    qsel = qsel_ref[...]
    page = k_refs[0].shape[1]
    s = jnp.concatenate([_dot(qsel, k_refs[i][...].astype(BF16)) for i in range(pp)], axis=1)
    m_prev = m_ref[...]
    m_new = jnp.maximum(m_prev, jnp.max(s, axis=-1, keepdims=True))
    alpha = jnp.exp2(m_prev - m_new)
    p = jnp.exp2(s - m_new)
    l_ref[...] = alpha * l_ref[...] + jnp.sum(p, axis=-1, keepdims=True)
    pb = p.astype(BF16)
    p_stack = jnp.concatenate([pb[:, i * page:(i + 1) * page] for i in range(pp)], axis=0)
    p_exp = _dot(p_stack, rexp_ref[...])
    rowe = lax.broadcasted_iota(jnp.int32, p_exp.shape, 0)
    cole = lax.broadcasted_iota(jnp.int32, p_exp.shape, 1)
    keep = (cole & (ATT_HEADS - 1)) == (rowe & (ATT_HEADS - 1))
    p2 = jnp.where(keep, p_exp, 0.0).astype(BF16)
    pv = _dot(p2[0:N_QROWS, :], v_refs[0][...].astype(BF16))
    for i in range(1, pp):
        pv = pv + _dot(p2[i * N_QROWS:(i + 1) * N_QROWS, :], v_refs[i][...].astype(BF16))
    acc_ref[...] = alpha * acc_ref[...] + pv
    m_ref[...] = m_new

    @pl.when(j == nj - 1)
    def _():
        kn = jnp.broadcast_to(kn_ref[0], (N_QROWS, QK_DIM))
        s_new = jnp.sum(qsel_ref[...].astype(F32) * kn, axis=-1, keepdims=True)
        m_p = m_ref[...]
        m_f = jnp.maximum(m_p, s_new)
        a_f = jnp.exp2(m_p - m_f)
        p_new = jnp.exp2(s_new - m_f)
        l_f = a_f * l_ref[...] + p_new
        vn = jnp.concatenate([vn_ref[0], vn_ref[0]], axis=0)
        o = (a_f * acc_ref[...] + p_new * vn) / l_f
        lam, lam_init = _lambda_from(lam_ref)
        od = o[0:ATT_HEADS, :] - lam * o[ATT_HEADS:N_QROWS, :]
        ms = jnp.mean(jnp.square(od), axis=-1, keepdims=True)
        on = od * lax.rsqrt(ms + NORM_EPS) * sw_ref[...] * (1.0 - lam_init)
        o_ref[0] = on * _silu(za_ref[0])


def _attn_sample(page_table, q, k_new, v_new, za, cache_k, cache_v, layer, lam_blk, sw, pp=8):
    db, n_pages = page_table.shape
    assert n_pages % pp == 0
    n_pool, page = cache_k.shape[1], cache_k.shape[2]
    r_np = np.zeros((page, page * ATT_HEADS), np.float32)
    for key in range(page):
        r_np[key, key * ATT_HEADS:(key + 1) * ATT_HEADS] = 1.0
    rexp = jnp.asarray(r_np, BF16)
    ck = jnp.transpose(cache_k, (0, 1, 3, 4, 5, 2)).reshape(DEPTH, n_pool, QK_DIM, page)
    cv = cache_v.reshape(DEPTH, n_pool, page * ATT_HEADS, ATT_V_DIM)
    tok = lambda w: pl.BlockSpec((1, 1, w), lambda b, j, pt: (b, 0, 0))
    tokh = pl.BlockSpec((1, ATT_HEADS, ATT_V_DIM), lambda b, j, pt: (b, 0, 0))
    full = lambda a: pl.BlockSpec(a.shape, lambda b, j, pt: (0, 0))

    def page_spec(i, rows, w):
        return pl.BlockSpec((None, None, rows, w), lambda b, j, pt: (layer, pt[b, j * pp + i], 0, 0))

    grid_spec = pltpu.PrefetchScalarGridSpec(
        num_scalar_prefetch=1,
        grid=(db, n_pages // pp),
        in_specs=[tok(QK_DIM), tok(QK_DIM), tokh, tokh, full(lam_blk), full(sw), full(rexp)]
        + [page_spec(i, QK_DIM, page) for i in range(pp)]
        + [page_spec(i, page * ATT_HEADS, ATT_V_DIM) for i in range(pp)],
        out_specs=tokh,
        scratch_shapes=[pltpu.VMEM((N_QROWS, QK_DIM), BF16), pltpu.VMEM((N_QROWS, 1), F32),
                        pltpu.VMEM((N_QROWS, 1), F32), pltpu.VMEM((N_QROWS, ATT_V_DIM), F32)],
    )
    r3 = lambda a: a.reshape(db, 1, a.shape[-1])
    rh = lambda a: a.reshape(db, ATT_HEADS, ATT_V_DIM)
    out = pl.pallas_call(
        functools.partial(_attn_sample_kernel, pp=pp),
        out_shape=jax.ShapeDtypeStruct((db, ATT_HEADS, ATT_V_DIM), F32),
        grid_spec=grid_spec,
        compiler_params=_cparams(("arbitrary", "arbitrary")),
        name="attn_sample",
    )(page_table, r3(q), r3(k_new), rh(v_new), rh(za), lam_blk, sw, rexp, *([ck] * pp), *([cv] * pp))
    return out.reshape(db, D_ATT)


def _pad_lanes(v, width=LANES):
    v = v.reshape(1, -1)
    return jnp.pad(v, ((0, 0), (0, width - v.shape[1])))


def _pick_tile(m, pref):
    t = min(pref, m)
    while m % t:
        t //= 2
    return t


def kernel(x_prompt, x_sample, cache_k, cache_v, state_ssm, state_conv, page_table, p_prompt, p_sample,
           w_in, conv_w, conv_b, dt_bias, a_log, d_skip, ssd_norm_w, lam_q1, lam_k1, lam_q2, lam_k2,
           subln_w, w_out, ln_g, ln_b, w_ple, w_ple_gate, b_ple_gate):
    b, l, _ = x_prompt.shape
    db = x_sample.shape[0]
    mp = b * l
    xp = x_prompt.reshape(mp, D_MODEL)
    xs = x_sample.reshape(db, D_MODEL)
    e_np = _head_expand_matrix()
    e_mat = jnp.asarray(e_np, BF16)
    e_mat_t = jnp.asarray(e_np.T, BF16)
    ts_in = _pick_tile(l, 256 // b)
    tm_out = _pick_tile(mp, 512)
    tq = _pick_tile(l, 1024)
    tk = tq // 2
    rs = _pick_tile(tk, 256)
    row = lambda v: v.reshape(1, -1)
    dt_lo = D_SSD + CONV_DIM
    w_all = jnp.concatenate(
        [w_in[:, :, :dt_lo], w_in[:, :, dt_lo + SSD_HEADS:], w_in[:, :, dt_lo:dt_lo + SSD_HEADS],
         jnp.zeros((DEPTH, D_MODEL, DT_PAD - SSD_HEADS), w_in.dtype)], axis=2).astype(BF16)
    pp_all = p_prompt.reshape(DEPTH, mp, PLE_DIM)
    ps_all = p_sample.reshape(DEPTH, db, PLE_DIM)

    kt_all = jnp.zeros((DEPTH, b, QK_DIM, l), F32)
    v_all = jnp.zeros((DEPTH, b, l, D_ATT), F32)
    hp, cp, ksl, vsl, hsl, csl = [], [], [], [], [], []
    for layer in range(DEPTH):
        lam_init = 0.8 - 0.6 * math.exp(-0.3 * layer)
        cw, cb = conv_w[layer], row(conv_b[layer])
        dtb, alog = _pad_lanes(dt_bias[layer]), _pad_lanes(a_log[layer])
        dsk = row(jnp.repeat(d_skip[layer], SSD_HEAD_DIM))
        nw = row(ssd_norm_w[layer])
        lam_blk = jnp.concatenate(
            [_pad_lanes(lam_q1[layer]), _pad_lanes(lam_k1[layer]), _pad_lanes(lam_q2[layer]),
             _pad_lanes(lam_k2[layer]), jnp.full((1, LANES), lam_init, F32), jnp.zeros((3, LANES), F32)], axis=0)
        sw = row(subln_w[layer])
        wo = w_out[layer].astype(BF16)
        wp = w_ple[layer].astype(BF16)
        wg = w_ple_gate[layer].astype(BF16)
        g, bb, bg = row(ln_g[layer]), row(ln_b[layer]), row(b_ple_gate[layer])

        z, xbc, dt, q, za, kb, vb, kt_all, v_all = _in_proj_prompt(
            xp.reshape(b, l, D_MODEL), w_all, layer, kt_all, v_all, ts_in)
        y_ssd, h_new = _ssd_prompt(xbc, dt, z, cw, cb, dt_bias[layer].reshape(SSD_HEADS, 1), alog, dsk, nw, e_mat, e_mat_t)
        y_att = _attn_prompt(q, kb, vb, za, lam_blk, sw, tq, tk, rs)
        xp = _layer_out(y_ssd.reshape(mp, D_SSD), y_att.reshape(mp, D_ATT), xp,
                        pp_all, layer, wo, g, bb, wp, wg, bg, tm_out)
        hp.append(h_new.reshape(b, SSD_HEADS, SSD_HEAD_DIM, SSD_STATE))
        cp.append(xbc[:, l - (SSD_CONV - 1):, :])

        z, xbc, dt, q, k, v, za, _, _ = _in_proj(xs, w_all, layer, db)
        y_ssd, conv_new, h_new = _ssd_sample(xbc, dt, z, state_conv, state_ssm, layer,
                                             cw, cb, dtb, alog, dsk, nw, e_mat)
        y_att = _attn_sample(page_table, q, k, v, za, cache_k, cache_v, layer, lam_blk, sw)
        xs = _layer_out(y_ssd, y_att, xs, ps_all, layer, wo, g, bb, wp, wg, bg, db)
        ksl.append(k.reshape(db, 1, ATT_HEADS, 2, ATT_HEAD_DIM))
        vsl.append(v.reshape(db, 1, ATT_HEADS, ATT_V_DIM))
        hsl.append(h_new.reshape(db, SSD_HEADS, SSD_HEAD_DIM, SSD_STATE))
        csl.append(conv_new)

    k_prompt = kt_all.reshape(DEPTH, b, ATT_HEADS, 2, ATT_HEAD_DIM, l).transpose(0, 1, 5, 2, 3, 4)
    v_prompt = v_all.reshape(DEPTH, b, l, ATT_HEADS, ATT_V_DIM)
    return (xp.reshape(b, l, D_MODEL), xs.reshape(db, 1, D_MODEL), k_prompt, v_prompt,
            jnp.stack(hp), jnp.stack(cp), jnp.stack(ksl), jnp.stack(vsl), jnp.stack(hsl), jnp.stack(csl))
```
